```python
import math
import jax, jax.numpy as jnp
from jax import lax
import numpy as np

D_MODEL = 1024
BATCH = 8
SEQ = 8192
DEPTH = 1
DEC_BATCH = 128
DEC_SEQ = 8
PAST_LEN = 8192
PAGE_SIZE = 128

N_HEADS_A = 4
HEAD_DIM_A = 64
V_DIM_A = 2 * HEAD_DIM_A
QK_WIDTH_A = N_HEADS_A * 2 * HEAD_DIM_A
WIDTH_A = N_HEADS_A * V_DIM_A
Q_BLOCK = 128
SSM_GROUP = 16
WIDTH_S = D_MODEL // 2
N_GROUPS_S = WIDTH_S // SSM_GROUP
STATE_DIM = 64
DT_MIN = 1e-3
DT_MAX = 1e-1
N_MEM = 256
N_HEADS_X = 4
HEAD_DIM_X = 128
WIDTH_X = N_HEADS_X * HEAD_DIM_X
D_FF = -(-8 * D_MODEL // (3 * 256)) * 256
SPLITS = (QK_WIDTH_A, 2 * QK_WIDTH_A, 2 * QK_WIDTH_A + WIDTH_A, 2 * QK_WIDTH_A + WIDTH_A + WIDTH_S)
N_IN = 2 * QK_WIDTH_A + WIDTH_A + WIDTH_S + 2 * D_MODEL
EPS = 1e-5

kernel_name = 'diffattn_s5_gated_hybrid_step'


def rmsnorm(x, g):
    xf = x.astype(jnp.float32)
    r = lax.rsqrt(jnp.mean(xf * xf, axis=-1, keepdims=True) + EPS)
    return (xf * r).astype(x.dtype) * g


def alibi_slopes(n):
    return jnp.exp2(-8.0 * jnp.arange(1, n + 1, dtype=jnp.float32) / n)


def diff_attend(q, k, v, q_pos, k_pos, lam, slopes):
    s = jnp.einsum('bqhmd,bkhmd->bhmqk', q, k, preferred_element_type=jnp.float32) * (HEAD_DIM_A ** -0.5)
    dist = (q_pos[:, None] - k_pos[None, :]).astype(jnp.float32)
    bias = jnp.where(dist >= 0, -slopes[:, None, None] * dist, -jnp.inf)
    p = jax.nn.softmax(s + bias[None, :, None], axis=-1)
    w = p[:, :, 0] - lam * p[:, :, 1]
    return jnp.einsum('bhqk,bkhe->bqhe', w.astype(v.dtype), v)


def prompt_diff_attention(q, k, v, lam, slopes):
    bsz, s_len = q.shape[:2]
    nb = s_len // Q_BLOCK
    qb = jnp.moveaxis(q.reshape(bsz, nb, Q_BLOCK, N_HEADS_A, 2, HEAD_DIM_A), 1, 0)
    k_pos = jnp.arange(s_len, dtype=jnp.int32)

    def one_block(args):
        qi, bi = args
        q_pos = bi * Q_BLOCK + jnp.arange(Q_BLOCK, dtype=jnp.int32)
        return diff_attend(qi, k, v, q_pos, k_pos, lam, slopes)

    out = lax.map(one_block, (qb, jnp.arange(nb, dtype=jnp.int32)))
    return jnp.moveaxis(out, 0, 1).reshape(bsz, s_len, N_HEADS_A, V_DIM_A)


def sample_diff_attention(q, k, v, lam, slopes, pool_k, pool_v, page_table):
    dbsz, t_len = q.shape[:2]
    past_len = page_table.shape[1] * pool_k.shape[1]
    past_k = pool_k[page_table].reshape(dbsz, past_len, N_HEADS_A, 2, HEAD_DIM_A)
    past_v = pool_v[page_table].reshape(dbsz, past_len, N_HEADS_A, V_DIM_A)
    k_all = jnp.concatenate([past_k.astype(k.dtype), k], axis=1)
    v_all = jnp.concatenate([past_v.astype(v.dtype), v], axis=1)
    q_pos = past_len + jnp.arange(t_len, dtype=jnp.int32)
    k_pos = jnp.arange(past_len + t_len, dtype=jnp.int32)
    return diff_attend(q, k_all, v_all, q_pos, k_pos, lam, slopes)


def _complex_affine_combine(e1, e2):
    a1r, a1i, b1r, b1i = e1
    a2r, a2i, b2r, b2i = e2
    return (a2r * a1r - a2i * a1i, a2r * a1i + a2i * a1r,
            a2r * b1r - a2i * b1i + b2r, a2r * b1i + a2i * b1r + b2i)


def s5_scan(u, x0_re, x0_im, a_re, a_im, b_re, b_im, c_re, c_im, d, log_dt):
    f32 = jnp.float32
    bsz, t_len, _ = u.shape
    uf = u.astype(f32).reshape(bsz, t_len, N_GROUPS_S, SSM_GROUP)
    a_re, a_im = a_re.astype(f32), a_im.astype(f32)
    dt = jnp.exp(log_dt.astype(f32))[:, None]
    mag = jnp.exp(dt * a_re)
    abar_re, abar_im = mag * jnp.cos(dt * a_im), mag * jnp.sin(dt * a_im)
    den = a_re * a_re + a_im * a_im
    f_re = ((abar_re - 1.0) * a_re + abar_im * a_im) / den
    f_im = (abar_im * a_re - (abar_re - 1.0) * a_im) / den
    b_re, b_im = b_re.astype(f32), b_im.astype(f32)
    bbar_re = f_re[..., None] * b_re - f_im[..., None] * b_im
    bbar_im = f_re[..., None] * b_im + f_im[..., None] * b_re
    bu_re = jnp.einsum('gpc,btgc->btgp', bbar_re, uf)
    bu_im = jnp.einsum('gpc,btgc->btgp', bbar_im, uf)
    seq_re = jnp.broadcast_to(abar_re, (1, t_len) + abar_re.shape)
    seq_im = jnp.broadcast_to(abar_im, (1, t_len) + abar_im.shape)
    acum_re, acum_im, s_re, s_im = lax.associative_scan(
        _complex_affine_combine, (seq_re, seq_im, bu_re, bu_im), axis=1)
    x0_re = x0_re.astype(f32)[:, None]
    x0_im = x0_im.astype(f32)[:, None]
    x_re = acum_re * x0_re - acum_im * x0_im + s_re
    x_im = acum_re * x0_im + acum_im * x0_re + s_im
    y = (jnp.einsum('gcp,btgp->btgc', c_re.astype(f32), x_re)
         - jnp.einsum('gcp,btgp->btgc', c_im.astype(f32), x_im))
    y = (y + d.astype(f32).reshape(N_GROUPS_S, SSM_GROUP) * uf).reshape(bsz, t_len, WIDTH_S)
    return y.astype(u.dtype), x_re[:, -1], x_im[:, -1]


def cross_attend(q, k, v):
    s = jnp.einsum('bqhd,bkhd->bhqk', q, k, preferred_element_type=jnp.float32) * (HEAD_DIM_X ** -0.5)
    p = jax.nn.softmax(s, axis=-1)
    return jnp.einsum('bhqk,bkhd->bqhd', p.astype(v.dtype), v)


def memory_kv(mem, g_mem, w_mem_kv):
    bsz = mem.shape[0]
    mk, mv = jnp.split(rmsnorm(mem, g_mem) @ w_mem_kv, 2, axis=-1)
    return (mk.reshape(bsz, N_MEM, N_HEADS_X, HEAD_DIM_X), mv.reshape(bsz, N_MEM, N_HEADS_X, HEAD_DIM_X))


def trunk_layer(x, li, params, attend, x0_re, x0_im, mem_k, mem_v):
    (g_mix, w_in, b_gate, lq1, lk1, lq2, lk2, g_subln, a_re, a_im, b_re, b_im, c_re, c_im,
     d, log_dt, w_glu, b_glu, w_proj_a, w_proj_s, w_out, g_cross, w_cq, w_co,
     g_ffn, w_ff_in, w_ff_out) = params
    f32 = jnp.float32
    bsz, t_len, _ = x.shape
    h = rmsnorm(x, g_mix)
    q, k, v, u, gates = jnp.split(h @ w_in, SPLITS, axis=-1)
    q = q.reshape(bsz, t_len, N_HEADS_A, 2, HEAD_DIM_A)
    k = k.reshape(bsz, t_len, N_HEADS_A, 2, HEAD_DIM_A)
    v = v.reshape(bsz, t_len, N_HEADS_A, V_DIM_A)
    lam_init = 0.8 - 0.6 * math.exp(-0.3 * li)
    lam = (jnp.exp(jnp.sum(lq1.astype(f32) * lk1.astype(f32)))
           - jnp.exp(jnp.sum(lq2.astype(f32) * lk2.astype(f32))) + lam_init)
    o_a = attend(q, k, v, lam)
    o_a = (rmsnorm(o_a, g_subln) * (1.0 - lam_init)).reshape(bsz, t_len, WIDTH_A)
    y_s, s_re, s_im = s5_scan(u, x0_re, x0_im, a_re, a_im, b_re, b_im, c_re, c_im, d, log_dt)
    z_s = jax.nn.gelu(y_s)
    o_s = z_s * jax.nn.sigmoid(z_s @ w_glu + b_glu)
    g_a, g_s = jnp.split(jax.nn.sigmoid(gates + b_gate), 2, axis=-1)
    x = x + (g_a * (o_a @ w_proj_a) + g_s * (o_s @ w_proj_s)) @ w_out
    h = rmsnorm(x, g_cross)
    qx = (h @ w_cq).reshape(bsz, t_len, N_HEADS_X, HEAD_DIM_X)
    x = x + cross_attend(qx, mem_k, mem_v).reshape(bsz, t_len, WIDTH_X) @ w_co
    h = rmsnorm(x, g_ffn)
    ff_a, ff_b = jnp.split(h @ w_ff_in, 2, axis=-1)
    x = x + (jax.nn.silu(ff_a) * ff_b) @ w_ff_out
    return x, k, v, s_re, s_im


def setup_inputs(seed: int = 0) -> dict:
    key = jax.random.key(seed)
    ks = iter(jax.random.split(key, 64))
    f32 = jnp.float32

    def nrm(shape, scale):
        return scale * jax.random.normal(next(ks), shape, f32)

    def gain(shape):
        return 1.0 + nrm(shape, 0.1)

    n_pages = PAST_LEN // PAGE_SIZE
    n_used = DEC_BATCH * n_pages
    n_phys = n_used + max(1, n_used // 4)
    page_table = jax.random.permutation(next(ks), n_phys)[:n_used].reshape(DEC_BATCH, n_pages).astype(jnp.int32)
    G, P = N_GROUPS_S, STATE_DIM
    a_re = -0.5 + nrm((DEPTH, G, P), 0.01)
    a_im = jnp.pi * jnp.arange(P, dtype=f32)[None, None, :] + nrm((DEPTH, G, P), 0.01)
    log_dt = jax.random.uniform(next(ks), (DEPTH, G), f32, math.log(DT_MIN), math.log(DT_MAX))
    return {
        'x_prompt': nrm((BATCH, SEQ, D_MODEL), 1.0),
        'x_sample': nrm((DEC_BATCH, DEC_SEQ, D_MODEL), 1.0),
        'mem_prompt': nrm((BATCH, N_MEM, D_MODEL), 1.0),
        'cache_k': nrm((DEPTH, n_phys, PAGE_SIZE, N_HEADS_A, 2, HEAD_DIM_A), 1.0),
        'cache_v': nrm((DEPTH, n_phys, PAGE_SIZE, N_HEADS_A, V_DIM_A), 1.0),
        'state_ssm_re': nrm((DEPTH, DEC_BATCH, G, P), 0.5),
        'state_ssm_im': nrm((DEPTH, DEC_BATCH, G, P), 0.5),
        'cache_mem_k': nrm((DEPTH, DEC_BATCH, N_MEM, N_HEADS_X, HEAD_DIM_X), 1.0),
        'cache_mem_v': nrm((DEPTH, DEC_BATCH, N_MEM, N_HEADS_X, HEAD_DIM_X), 1.0),
        'page_table': page_table,
        'g_mix': gain((DEPTH, D_MODEL)),
        'w_in': nrm((DEPTH, D_MODEL, N_IN), D_MODEL ** -0.5),
        'b_gate': nrm((DEPTH, 2 * D_MODEL), 0.02),
        'lambda_q1': nrm((DEPTH, HEAD_DIM_A), 0.1),
        'lambda_k1': nrm((DEPTH, HEAD_DIM_A), 0.1),
        'lambda_q2': nrm((DEPTH, HEAD_DIM_A), 0.1),
        'lambda_k2': nrm((DEPTH, HEAD_DIM_A), 0.1),
        'g_subln': gain((DEPTH, V_DIM_A)),
        'ssm_a_re': a_re,
        'ssm_a_im': a_im,
        'ssm_b_re': nrm((DEPTH, G, P, SSM_GROUP), (2 * SSM_GROUP) ** -0.5),
        'ssm_b_im': nrm((DEPTH, G, P, SSM_GROUP), (2 * SSM_GROUP) ** -0.5),
        'ssm_c_re': nrm((DEPTH, G, SSM_GROUP, P), (2 * P) ** -0.5),
        'ssm_c_im': nrm((DEPTH, G, SSM_GROUP, P), (2 * P) ** -0.5),
        'ssm_d': nrm((DEPTH, WIDTH_S), 1.0),
        'ssm_log_dt': log_dt,
        'w_glu': nrm((DEPTH, WIDTH_S, WIDTH_S), WIDTH_S ** -0.5),
        'b_glu': nrm((DEPTH, WIDTH_S), 0.02),
        'w_proj_a': nrm((DEPTH, WIDTH_A, D_MODEL), WIDTH_A ** -0.5),
        'w_proj_s': nrm((DEPTH, WIDTH_S, D_MODEL), WIDTH_S ** -0.5),
        'w_out': nrm((DEPTH, D_MODEL, D_MODEL), D_MODEL ** -0.5),
        'g_cross': gain((DEPTH, D_MODEL)),
        'g_mem': gain((DEPTH, D_MODEL)),
        'w_cq': nrm((DEPTH, D_MODEL, WIDTH_X), D_MODEL ** -0.5),
        'w_mem_kv': nrm((DEPTH, D_MODEL, 2 * WIDTH_X), D_MODEL ** -0.5),
        'w_co': nrm((DEPTH, WIDTH_X, D_MODEL), WIDTH_X ** -0.5),
        'g_ffn': gain((DEPTH, D_MODEL)),
        'w_ff_in': nrm((DEPTH, D_MODEL, 2 * D_FF), D_MODEL ** -0.5),
        'w_ff_out': nrm((DEPTH, D_FF, D_MODEL), D_FF ** -0.5),
        'g_final': gain((D_MODEL,)),
    }


def reference(x_prompt, x_sample, mem_prompt, cache_k, cache_v, state_ssm_re, state_ssm_im,
              cache_mem_k, cache_mem_v, page_table, g_mix, w_in, b_gate, lambda_q1, lambda_k1,
              lambda_q2, lambda_k2, g_subln, ssm_a_re, ssm_a_im, ssm_b_re, ssm_b_im, ssm_c_re,
              ssm_c_im, ssm_d, ssm_log_dt, w_glu, b_glu, w_proj_a, w_proj_s, w_out, g_cross,
              g_mem, w_cq, w_mem_kv, w_co, g_ffn, w_ff_in, w_ff_out, g_final):
    slopes = alibi_slopes(N_HEADS_A)
    xp, xs = x_prompt, x_sample
    kp_l, vp_l, ks_l, vs_l = [], [], [], []
    srp_l, sip_l, srs_l, sis_l, mk_l, mv_l = [], [], [], [], [], []
    for li in range(DEPTH):
        params = (g_mix[li], w_in[li], b_gate[li], lambda_q1[li], lambda_k1[li], lambda_q2[li],
                  lambda_k2[li], g_subln[li], ssm_a_re[li], ssm_a_im[li], ssm_b_re[li], ssm_b_im[li],
                  ssm_c_re[li], ssm_c_im[li], ssm_d[li], ssm_log_dt[li], w_glu[li], b_glu[li],
                  w_proj_a[li], w_proj_s[li], w_out[li], g_cross[li], w_cq[li], w_co[li],
                  g_ffn[li], w_ff_in[li], w_ff_out[li])
        mk, mv = memory_kv(mem_prompt, g_mem[li], w_mem_kv[li])
        zeros = jnp.zeros((xp.shape[0], N_GROUPS_S, STATE_DIM), jnp.float32)
        xp, kp, vp, srp, sip = trunk_layer(
            xp, li, params, lambda q, k, v, lam: prompt_diff_attention(q, k, v, lam, slopes),
            zeros, zeros, mk, mv)
        pool_k, pool_v = cache_k[li], cache_v[li]
        xs, kn, vn, srs, sis = trunk_layer(
            xs, li, params,
            lambda q, k, v, lam: sample_diff_attention(q, k, v, lam, slopes, pool_k, pool_v, page_table),
            state_ssm_re[li], state_ssm_im[li], cache_mem_k[li].astype(xs.dtype), cache_mem_v[li].astype(xs.dtype))
        kp_l.append(kp); vp_l.append(vp); ks_l.append(kn); vs_l.append(vn)
        srp_l.append(srp); sip_l.append(sip); srs_l.append(srs); sis_l.append(sis)
        mk_l.append(mk); mv_l.append(mv)
    y_prompt = rmsnorm(xp, g_final)
    y_sample = rmsnorm(xs, g_final)
    return (y_prompt, y_sample,
            jnp.stack(kp_l), jnp.stack(vp_l), jnp.stack(ks_l), jnp.stack(vs_l),
            jnp.stack(srp_l), jnp.stack(sip_l), jnp.stack(srs_l), jnp.stack(sis_l),
            jnp.stack(mk_l), jnp.stack(mv_l))
```

```python
import functools
import math

import numpy as np
import jax
import jax.numpy as jnp
from jax import lax
from jax.experimental import pallas as pl
from jax.experimental.pallas import tpu as pltpu

F32 = jnp.float32
BF16 = jnp.bfloat16
EPS = 1e-5
LAM_INIT = 0.8 - 0.6 * math.exp(-0.3 * 0)

V7X_LANES = 128
V7X_SUBLANES = 8
V7X_VMEM_LIMIT_BYTES = 48 * 1024 * 1024

IN_PROJ_TOKENS = 512
ATTN_Q_BLOCK = 512
SAMPLE_PAGES_PER_STEP = 8
S5_CHUNK_ROWS = 512
MERGE_TOKENS = 512
FFN_TOKENS = 256
CROSS_TOKENS = 512


def _params(*sem):
    return pltpu.CompilerParams(dimension_semantics=sem,
                                vmem_limit_bytes=V7X_VMEM_LIMIT_BYTES)


def _rmsnorm(x, g):
    r = lax.rsqrt(jnp.mean(x * x, axis=-1, keepdims=True) + EPS)
    return (x * r) * g


def _dot(a, b):
    return jnp.dot(a, b, preferred_element_type=F32)


def _dot_nt(a, b):
    return lax.dot_general(a, b, (((1,), (1,)), ((), ())), preferred_element_type=F32)


def _full(shape):
    zeros = (0,) * len(shape)
    return pl.BlockSpec(shape, lambda *_: zeros)


def _norm_matmul_kernel(x_ref, g_ref, w_ref, o_ref):
    h = _rmsnorm(x_ref[...], g_ref[...]).astype(BF16)
    o_ref[...] = _dot(h, w_ref[...])


def _norm_matmul(x, g, w_bf16, tm):
    n, d = x.shape
    n_out = w_bf16.shape[1]
    return pl.pallas_call(
        _norm_matmul_kernel,
        grid=(n // tm,),
        in_specs=[pl.BlockSpec((tm, d), lambda i: (i, 0)), _full((1, d)), _full((d, n_out))],
        out_specs=pl.BlockSpec((tm, n_out), lambda i: (i, 0)),
        out_shape=jax.ShapeDtypeStruct((n, n_out), F32),
        compiler_params=_params("parallel"),
        name="norm_matmul",
    )(x, g.reshape(1, d), w_bf16)


def _in_proj_prompt_kernel(x_ref, g_ref, wkvu_ref, wqvt_ref,
                           k32_ref, v32_ref, kbf_ref, u_ref, qt_ref, vt_ref, *, wa):
    h = _rmsnorm(x_ref[...], g_ref[...]).astype(BF16)
    r = _dot(h, wkvu_ref[...])
    k = r[:, :wa]
    k32_ref[...] = k
    kbf_ref[...] = k.astype(BF16)
    v32_ref[...] = r[:, wa:2 * wa]
    u_ref[...] = r[:, 2 * wa:]
    rt = _dot_nt(wqvt_ref[...], h)
    qt_ref[...] = rt[:wa].astype(BF16)
    vt_ref[...] = rt[wa:].astype(BF16)


def _in_proj_prompt(x, g, wkvu, wqvt, wa, ws):
    b, t, d = x.shape
    tm = IN_PROJ_TOKENS
    row = lambda bi, i: (bi, i, 0)
    col = lambda bi, i: (bi, 0, i)
    return pl.pallas_call(
        functools.partial(_in_proj_prompt_kernel, wa=wa),
        grid=(b, t // tm),
        in_specs=[pl.BlockSpec((None, tm, d), row), _full((1, d)),
                  _full(wkvu.shape), _full(wqvt.shape)],
        out_specs=[pl.BlockSpec((None, tm, wa), row), pl.BlockSpec((None, tm, wa), row),
                   pl.BlockSpec((None, tm, wa), row), pl.BlockSpec((None, tm, ws), row),
                   pl.BlockSpec((None, wa, tm), col), pl.BlockSpec((None, wa, tm), col)],
        out_shape=[jax.ShapeDtypeStruct((b, t, wa), F32), jax.ShapeDtypeStruct((b, t, wa), F32),
                   jax.ShapeDtypeStruct((b, t, wa), BF16), jax.ShapeDtypeStruct((b, t, ws), F32),
                   jax.ShapeDtypeStruct((b, wa, t), BF16), jax.ShapeDtypeStruct((b, wa, t), BF16)],
        compiler_params=_params("parallel", "parallel"),
        name="in_proj_prompt",
    )(x, g.reshape(1, d), wkvu, wqvt)


def _in_proj_sample_kernel(x_ref, g_ref, w_ref, q_ref, k_ref, v_ref, u_ref, *, wa):
    h = _rmsnorm(x_ref[...], g_ref[...]).astype(BF16)
    r = _dot(h, w_ref[...])
    q_ref[...] = r[:, :wa]
    k_ref[...] = r[:, wa:2 * wa]
    v_ref[...] = r[:, 2 * wa:3 * wa]
    u_ref[...] = r[:, 3 * wa:]


def _in_proj_sample(x, g, w, wa, ws):
    n, d = x.shape
    shapes = [jax.ShapeDtypeStruct((n, wa), F32)] * 3 + [jax.ShapeDtypeStruct((n, ws), F32)]
    return pl.pallas_call(
        functools.partial(_in_proj_sample_kernel, wa=wa),
        grid=(1,),
        in_specs=[_full((n, d)), _full((1, d)), _full(w.shape)],
        out_specs=[_full(s.shape) for s in shapes],
        out_shape=shapes,
        compiler_params=_params("arbitrary"),
        name="in_proj_sample",
    )(x, g.reshape(1, d), w)


def _lambda_full(lam_ref):
    lv = lam_ref[...]
    e1 = jnp.exp(jnp.sum(lv[0:1] * lv[1:2], axis=-1, keepdims=True))
    e2 = jnp.exp(jnp.sum(lv[2:3] * lv[3:4], axis=-1, keepdims=True))
    return e1 - e2 + LAM_INIT


def _attn_prompt_kernel(slopes_ref, qt_ref, k_ref, vt_ref, lam_ref, gsub_ref, o_ref,
                        acc0_ref, acc1_ref, *, blk, dk):
    i = pl.program_id(2)
    slope = slopes_ref[pl.program_id(1)]
    qt = qt_ref[...]
    zero = jnp.zeros((dk, blk), BF16)
    q0 = jnp.concatenate([qt[:dk], zero], axis=0)
    q1 = jnp.concatenate([zero, qt[dk:]], axis=0)
    krel = lax.broadcasted_iota(jnp.int32, (blk, blk), 0)
    kbias = slope * krel.astype(F32)
    causal = krel <= lax.broadcasted_iota(jnp.int32, (blk, blk), 1)

    def scores(j, masked):
        k = k_ref[pl.ds(pl.multiple_of(j * blk, blk), blk), :]
        s0 = _dot(k, q0) + kbias
        s1 = _dot(k, q1) + kbias
        if masked:
            s0 = jnp.where(causal, s0, -jnp.inf)
            s1 = jnp.where(causal, s1, -jnp.inf)
        return s0, s1

    def update(s, off, m, l, acc_ref, vt, first):
        mb = jnp.max(s, axis=0, keepdims=True) + off
        m_new = mb if first else jnp.maximum(m, mb)
        p = jnp.exp(s - (m_new - off))
        ps = jnp.sum(p, axis=0, keepdims=True)
        pv = _dot(vt, p.astype(BF16))
        if first:
            acc_ref[...] = pv
            return m_new, ps
        alpha = jnp.exp(m - m_new)
        acc_ref[...] = alpha * acc_ref[...] + pv
        return m_new, alpha * l + ps

    def block(j, carry, masked, first):
        m0, l0, m1, l1 = carry
        off = slope * ((j - i) * blk).astype(F32)
        vt = vt_ref[:, pl.ds(pl.multiple_of(j * blk, blk), blk)]
        s0, s1 = scores(j, masked)
        m0, l0 = update(s0, off, m0, l0, acc0_ref, vt, first)
        m1, l1 = update(s1, off, m1, l1, acc1_ref, vt, first)
        return m0, l0, m1, l1

    init = (jnp.zeros((1, blk), F32),) * 4
    carry = block(i, init, True, True)
    m0, l0, m1, l1 = lax.fori_loop(0, i, lambda j, c: block(j, c, False, False), carry)

    lam = _lambda_full(lam_ref)
    o = acc0_ref[...] / l0 - lam * (acc1_ref[...] / l1)
    r = lax.rsqrt(jnp.mean(o * o, axis=0, keepdims=True) + EPS)
    o = (o * r) * gsub_ref[...] * (1.0 - LAM_INIT)
    o_ref[...] = o.T.astype(o_ref.dtype)


def _attn_prompt(qt, kbf, vt, lam_vecs, g_subln, slopes, heads):
    b, wa, t = qt.shape
    dv = wa // heads
    dk = dv // 2
    blk = ATTN_Q_BLOCK
    return pl.pallas_call(
        functools.partial(_attn_prompt_kernel, blk=blk, dk=dk),
        grid=(b, heads, t // blk),
        in_specs=[pl.BlockSpec(memory_space=pltpu.SMEM),
                  pl.BlockSpec((None, dv, blk), lambda bi, h, i: (bi, h, i)),
                  pl.BlockSpec((None, t, dv), lambda bi, h, i: (bi, 0, h)),
                  pl.BlockSpec((None, dv, t), lambda bi, h, i: (bi, h, 0)),
                  _full(lam_vecs.shape), _full((dv, 1))],
        out_specs=pl.BlockSpec((None, blk, dv), lambda bi, h, i: (bi, i, h)),
        out_shape=jax.ShapeDtypeStruct((b, t, wa), BF16),
        scratch_shapes=[pltpu.VMEM((dv, blk), F32), pltpu.VMEM((dv, blk), F32)],
        compiler_params=_params("parallel", "parallel", "arbitrary"),
        name="attn_prompt",
    )(slopes, qt, kbf, vt, lam_vecs, g_subln.reshape(dv, 1))


def _attn_sample_kernel(pt_ref, *refs, pps, heads, tq, past_len, page):
    k_pages = refs[:pps]
    v_pages = refs[pps:2 * pps]
    (q_ref, kn_ref, vn_ref, slope_ref, qidx_ref, qmask_ref, lam_ref, gsub_ref, o_ref,
     qbd_ref, kpad_ref, vpad_ref, m_ref, l_ref, acc_ref) = refs[2 * pps:]
    c = pl.program_id(1)
    rows = heads * 2 * tq
    wa = q_ref.shape[-1]
    dv = wa // heads

    slope = slope_ref[...]
    qidx = qidx_ref[...]
    lane = lax.broadcasted_iota(jnp.int32, (rows, page), 1)

    def fold(s, vals):
        m_old = m_ref[...]
        m_new = jnp.maximum(m_old, jnp.max(s, axis=-1, keepdims=True))
        p = jnp.exp(s - m_new)
        alpha = jnp.exp(m_old - m_new)
        l_ref[...] = alpha * l_ref[...] + jnp.sum(p, axis=-1, keepdims=True)
        pv = _dot(p[:, :page].astype(BF16), vals[0])
        for j in range(1, len(vals)):
            pv += _dot(p[:, j * page:(j + 1) * page].astype(BF16), vals[j])
        acc_ref[...] = alpha * acc_ref[...] + pv
        m_ref[...] = m_new

    @pl.when(c == 0)
    def _():
        q = q_ref[...]
        qt = jnp.concatenate([q] * (rows // tq), axis=0)
        qbd_ref[...] = (qt * qmask_ref[...]).astype(BF16)
        kpad_ref[...] = jnp.zeros_like(kpad_ref)
        vpad_ref[...] = jnp.zeros_like(vpad_ref)
        kpad_ref[0:tq, :] = kn_ref[...]
        vpad_ref[0:tq, :] = vn_ref[...]
        s = _dot_nt(qbd_ref[...], kpad_ref[...].astype(BF16))
        s = s - slope * (qidx - lane).astype(F32)
        s = jnp.where(lane <= qidx, s, -jnp.inf)
        m = jnp.max(s, axis=-1, keepdims=True)
        p = jnp.exp(s - m)
        m_ref[...] = m
        l_ref[...] = jnp.sum(p, axis=-1, keepdims=True)
        acc_ref[...] = _dot(p.astype(BF16), vpad_ref[...].astype(BF16))

    qbd = qbd_ref[...]
    qpos = past_len + qidx
    parts = []
    for j in range(pps):
        s = _dot_nt(qbd, k_pages[j][...].astype(BF16))
        kpos = (c * pps + j) * page + lane
        parts.append(s - slope * (qpos - kpos).astype(F32))
    fold(jnp.concatenate(parts, axis=1), [v_pages[j][...].astype(BF16) for j in range(pps)])

    @pl.when(c == pl.num_programs(1) - 1)
    def _():
        lam = _lambda_full(lam_ref)
        o = acc_ref[...] / l_ref[...]
        for h in range(heads):
            r0 = h * 2 * tq
            o0 = o[r0:r0 + tq, h * dv:(h + 1) * dv]
            o1 = o[r0 + tq:r0 + 2 * tq, h * dv:(h + 1) * dv]
            d = o0 - lam * o1
            d = _rmsnorm(d, gsub_ref[...]) * (1.0 - LAM_INIT)
            o_ref[:, h * dv:(h + 1) * dv] = d.astype(o_ref.dtype)


def _attn_sample(q, k_new, v_new, pool_k, pool_v, page_table, lam_vecs, g_subln, slopes_np, heads):
    db, tq, wa = q.shape
    n_phys, page = pool_k.shape[0], pool_k.shape[1]
    n_pages = page_table.shape[1]
    past_len = n_pages * page
    pps = SAMPLE_PAGES_PER_STEP
    dv = wa // heads
    rows = heads * 2 * tq
    pk = pool_k.reshape(n_phys, page, wa)
    pv = pool_v.reshape(n_phys, page, wa)
    dk = dv // 2
    r_idx = np.arange(rows)
    slope_rows = jnp.asarray(np.repeat(np.asarray(slopes_np)[r_idx // (2 * tq)][:, None], page, 1), F32)
    qidx_rows = jnp.asarray(np.repeat((r_idx % tq)[:, None], page, 1), jnp.int32)
    qmask = jnp.asarray((r_idx // tq)[:, None] == (np.arange(wa) // dk)[None, :], F32)

    def page_spec(j):
        return pl.BlockSpec((None, page, wa),
                            lambda b, c, pt: (pt[b * n_pages + c * pps + j], 0, 0))

    per_b = lambda b, c, pt: (b, 0, 0)
    const2 = lambda b, c, pt: (0, 0)
    grid_spec = pltpu.PrefetchScalarGridSpec(
        num_scalar_prefetch=1,
        grid=(db, n_pages // pps),
        in_specs=([page_spec(j) for j in range(pps)] + [page_spec(j) for j in range(pps)]
                  + [pl.BlockSpec((None, tq, wa), per_b)] * 3
                  + [pl.BlockSpec((rows, page), const2), pl.BlockSpec((rows, page), const2),
                     pl.BlockSpec((rows, wa), const2),
                     pl.BlockSpec(lam_vecs.shape, const2),
                     pl.BlockSpec((1, dv), const2)]),
        out_specs=pl.BlockSpec((None, tq, wa), per_b),
        scratch_shapes=[pltpu.VMEM((rows, wa), BF16),
                        pltpu.VMEM((page, wa), F32), pltpu.VMEM((page, wa), F32),
                        pltpu.VMEM((rows, 1), F32), pltpu.VMEM((rows, 1), F32),
                        pltpu.VMEM((rows, wa), F32)])
    return pl.pallas_call(
        functools.partial(_attn_sample_kernel, pps=pps, heads=heads, tq=tq,
                          past_len=past_len, page=page),
        grid_spec=grid_spec,
        out_shape=jax.ShapeDtypeStruct((db, tq, wa), F32),
        compiler_params=_params("parallel", "arbitrary"),
        name="attn_sample",
    )(page_table.reshape(-1), *([pk] * pps), *([pv] * pps), q, k_new, v_new,
      slope_rows, qidx_rows, qmask, lam_vecs, g_subln.reshape(1, dv))


def _s5_kernel(u_ref, x0_ref, bd_ref, cd_ref, a_ref, d_ref, y_ref, xf_ref, xs_ref, st_ref,
               *, nb, steps, half_in, half_state):
    c = pl.program_id(0)

    @pl.when(c == 0)
    def _():
        st_ref[...] = x0_ref[...]

    u = u_ref[...]
    ub = u.astype(BF16)
    hs = half_state
    for k in range(2):
        xs_ref[:, 2 * k * hs:2 * (k + 1) * hs] = _dot(
            ub[:, k * half_in:(k + 1) * half_in], bd_ref[k])

    sub = V7X_SUBLANES
    for k in range(2):
        re0, im0 = 2 * k * hs, (2 * k + 1) * hs
        a_re = a_ref[2 * k]
        a_im = a_ref[2 * k + 1]

        def group(r, _):
            rows0 = pl.multiple_of(r * sub, sub)
            x_re = st_ref[pl.ds(rows0, sub), re0:re0 + hs]
            x_im = st_ref[pl.ds(rows0, sub), im0:im0 + hs]

            def step(t, carry):
                x_re, x_im = carry
                row = pl.multiple_of(t * nb + rows0, sub)
                n_re = a_re * x_re - a_im * x_im + xs_ref[pl.ds(row, sub), re0:re0 + hs]
                n_im = a_re * x_im + a_im * x_re + xs_ref[pl.ds(row, sub), im0:im0 + hs]
                xs_ref[pl.ds(row, sub), re0:re0 + hs] = n_re
                xs_ref[pl.ds(row, sub), im0:im0 + hs] = n_im
                return n_re, n_im

            x_re, x_im = lax.fori_loop(0, steps, step, (x_re, x_im))
            st_ref[pl.ds(rows0, sub), re0:re0 + hs] = x_re
            st_ref[pl.ds(rows0, sub), im0:im0 + hs] = x_im
            return 0

        lax.fori_loop(0, nb // sub, group, 0)

    ys = [_dot(xs_ref[:, 2 * k * hs:2 * (k + 1) * hs].astype(BF16), cd_ref[k]) for k in range(2)]
    y_ref[...] = jnp.concatenate(ys, axis=1) + d_ref[...] * u

    @pl.when(c == pl.num_programs(0) - 1)
    def _():
        xf_ref[...] = st_ref[...]


def _s5(u_tm, x0, bd, cd, a_coef, d_vec, nb):
    rows_total, width = u_tm.shape
    steps = min(rows_total // nb, max(1, S5_CHUNK_ROWS // nb))
    rows = steps * nb
    n_state = x0.shape[1]
    return pl.pallas_call(
        functools.partial(_s5_kernel, nb=nb, steps=steps, half_in=width // 2,
                          half_state=n_state // 4),
        grid=(rows_total // rows,),
        in_specs=[pl.BlockSpec((rows, width), lambda c: (c, 0)), _full(x0.shape),
                  _full(bd.shape), _full(cd.shape), _full(a_coef.shape), _full((1, width))],
        out_specs=[pl.BlockSpec((rows, width), lambda c: (c, 0)), _full(x0.shape)],
        out_shape=[jax.ShapeDtypeStruct((rows_total, width), F32),
                   jax.ShapeDtypeStruct(x0.shape, F32)],
        scratch_shapes=[pltpu.VMEM((rows, n_state), F32), pltpu.VMEM(x0.shape, F32)],
        compiler_params=_params("arbitrary"),
        name="s5_scan",
    )(u_tm, x0, bd, cd, a_coef, d_vec.reshape(1, width))


def _s5_weights(a_re, a_im, b_re, b_im, c_re, c_im, log_dt):
    g, p, ch = b_re.shape
    dt = jnp.exp(log_dt)[:, None]
    mag = jnp.exp(dt * a_re)
    abar_re, abar_im = mag * jnp.cos(dt * a_im), mag * jnp.sin(dt * a_im)
    den = a_re * a_re + a_im * a_im
    f_re = ((abar_re - 1.0) * a_re + abar_im * a_im) / den
    f_im = (abar_im * a_re - (abar_re - 1.0) * a_im) / den
    bbar_re = f_re[..., None] * b_re - f_im[..., None] * b_im
    bbar_im = f_re[..., None] * b_im + f_im[..., None] * b_re
    gh = g // 2
    eye = jnp.eye(gh, dtype=F32)

    def bd_half(w):
        return jnp.einsum('gpc,gh->gchp', w, eye).reshape(gh * ch, gh * p)

    def cd_half(w):
        return jnp.einsum('gcp,gh->gphc', w, eye).reshape(gh * p, gh * ch)

    bd = jnp.stack([jnp.concatenate([bd_half(bbar_re[k * gh:(k + 1) * gh]),
                                     bd_half(bbar_im[k * gh:(k + 1) * gh])], axis=1)
                    for k in range(2)]).astype(BF16)
    cd = jnp.stack([jnp.concatenate([cd_half(c_re[k * gh:(k + 1) * gh]),
                                     cd_half(-c_im[k * gh:(k + 1) * gh])], axis=0)
                    for k in range(2)]).astype(BF16)
    coef = [w[k * gh:(k + 1) * gh].reshape(1, gh * p) for k in range(2) for w in (abar_re, abar_im)]
    a_coef = jnp.broadcast_to(jnp.stack(coef), (4, V7X_SUBLANES, gh * p))
    return bd, cd, a_coef


def _state_pack(x_re, x_im):
    nb, g, p = x_re.shape
    hs = g * p // 2
    re = x_re.reshape(nb, 2, hs)
    im = x_im.reshape(nb, 2, hs)
    return jnp.stack([re[:, 0], im[:, 0], re[:, 1], im[:, 1]], axis=1).reshape(nb, 4 * hs)


def _state_unpack(x, g, p):
    nb = x.shape[0]
    hs = g * p // 2
    x = x.reshape(nb, 4, hs)
    re = jnp.stack([x[:, 0], x[:, 2]], axis=1).reshape(nb, g, p)
    im = jnp.stack([x[:, 1], x[:, 3]], axis=1).reshape(nb, g, p)
    return re, im


def _merge_kernel(x_ref, oa_ref, ys_ref, gmix_ref, wg_ref, bg_ref, wglu_ref, bglu_ref,
                  wpa_ref, wps_ref, wout_ref, gcross_ref, wcq_ref, x1_ref, qx_ref, *, qscale):
    x = x_ref[...]
    d = x.shape[-1]
    h = _rmsnorm(x, gmix_ref[...]).astype(BF16)
    gate = jax.nn.sigmoid(_dot(h, wg_ref[...]) + bg_ref[...])
    z = jax.nn.gelu(ys_ref[...])
    o_s = z * jax.nn.sigmoid(_dot(z.astype(BF16), wglu_ref[...]) + bglu_ref[...])
    pa = _dot(oa_ref[...].astype(BF16), wpa_ref[...])
    ps = _dot(o_s.astype(BF16), wps_ref[...])
    mix = gate[:, :d] * pa + gate[:, d:] * ps
    x1 = x + _dot(mix.astype(BF16), wout_ref[...])
    x1_ref[...] = x1
    hc = _rmsnorm(x1, gcross_ref[...]).astype(BF16)
    qx_ref[...] = (_dot(hc, wcq_ref[...]) * qscale).astype(qx_ref.dtype)


def _merge(x, oa, ys, ys_spec, grid, tok_spec, weights, qscale, inter_dtype):
    (gmix, wg, bg, wglu, bglu, wpa, wps, wout, gcross, wcq) = weights
    d = x.shape[-1]
    wx = wcq.shape[1]
    wspecs = [_full(w.shape) for w in weights]
    return pl.pallas_call(
        functools.partial(_merge_kernel, qscale=qscale),
        grid=grid,
        in_specs=[tok_spec(d), tok_spec(oa.shape[-1]), ys_spec] + wspecs,
        out_specs=[tok_spec(d), tok_spec(wx)],
        out_shape=[jax.ShapeDtypeStruct(x.shape, F32),
                   jax.ShapeDtypeStruct(x.shape[:-1] + (wx,), inter_dtype)],
        compiler_params=_params(*(("parallel",) * len(grid))),
        name="merge",
    )(x, oa, ys, *weights)


def _cross_kernel(q_ref, k_ref, v_ref, o_ref, *, heads):
    dx = q_ref.shape[-1] // heads
    for h in range(heads):
        sl = slice(h * dx, (h + 1) * dx)
        q = q_ref[:, sl].astype(BF16)
        s = _dot_nt(q, k_ref[:, sl].astype(BF16))
        m = jnp.max(s, axis=-1, keepdims=True)
        p = jnp.exp(s - m)
        l = jnp.sum(p, axis=-1, keepdims=True)
        o = _dot(p.astype(BF16), v_ref[:, sl].astype(BF16)) / l
        o_ref[:, sl] = o.astype(o_ref.dtype)


def _cross(qx, mem_k, mem_v, heads, tm):
    b, t, wx = qx.shape
    n_mem = mem_k.shape[1]
    tok = pl.BlockSpec((None, tm, wx), lambda bi, i: (bi, i, 0))
    mem = pl.BlockSpec((None, n_mem, wx), lambda bi, i: (bi, 0, 0))
    return pl.pallas_call(
        functools.partial(_cross_kernel, heads=heads),
        grid=(b, t // tm),
        in_specs=[tok, mem, mem],
        out_specs=tok,
        out_shape=jax.ShapeDtypeStruct(qx.shape, qx.dtype),
        compiler_params=_params("parallel", "parallel"),
        name="cross_attn",
    )(qx, mem_k, mem_v)


def _ffn_kernel(x_ref, ca_ref, wco_ref, gffn_ref, wfi_ref, wfo_ref, gfin_ref, y_ref, *, dff):
    x2 = x_ref[...] + _dot(ca_ref[...].astype(BF16), wco_ref[...])
    h = _rmsnorm(x2, gffn_ref[...]).astype(BF16)
    ff = _dot(h, wfi_ref[...])
    hid = jax.nn.silu(ff[:, :dff]) * ff[:, dff:]
    x3 = x2 + _dot(hid.astype(BF16), wfo_ref[...])
    y_ref[...] = _rmsnorm(x3, gfin_ref[...])


def _ffn(x1, ca, weights, tm):
    (wco, gffn, wfi, wfo, gfin) = weights
    n, d = x1.shape
    tm = min(tm, n)
    tok = lambda w: pl.BlockSpec((tm, w), lambda i: (i, 0))
    return pl.pallas_call(
        functools.partial(_ffn_kernel, dff=wfo.shape[0]),
        grid=(n // tm,),
        in_specs=[tok(d), tok(ca.shape[-1])] + [_full(w.shape) for w in weights],
        out_specs=tok(d),
        out_shape=jax.ShapeDtypeStruct((n, d), F32),
        compiler_params=_params("parallel"),
        name="ffn",
    )(x1, ca, *weights)


def kernel(x_prompt, x_sample, mem_prompt, cache_k, cache_v, state_ssm_re, state_ssm_im, cache_mem_k, cache_mem_v, page_table, g_mix, w_in, b_gate, lambda_q1, lambda_k1, lambda_q2, lambda_k2, g_subln, ssm_a_re, ssm_a_im, ssm_b_re, ssm_b_im, ssm_c_re, ssm_c_im, ssm_d, ssm_log_dt, w_glu, b_glu, w_proj_a, w_proj_s, w_out, g_cross, g_mem, w_cq, w_mem_kv, w_co, g_ffn, w_ff_in, w_ff_out, g_final):
    assert g_mix.shape[0] == 1, "single-layer trunk"
    b, t, d = x_prompt.shape
    db, dt_len, _ = x_sample.shape
    heads = cache_k.shape[3]
    dk = cache_k.shape[5]
    dv = cache_v.shape[4]
    wa = heads * dv
    g, p, _ = ssm_b_re.shape[1:]
    ws = ssm_d.shape[1]
    n_mem, heads_x, dx = cache_mem_k.shape[2:]
    wx = heads_x * dx
    row2 = lambda v: v.reshape(1, -1)

    w_in0 = w_in[0]
    qs = dk ** -0.5
    w_q, w_k, w_v, w_u = (w_in0[:, :wa] * qs, w_in0[:, wa:2 * wa],
                          w_in0[:, 2 * wa:3 * wa], w_in0[:, 3 * wa:3 * wa + ws])
    w_gate = w_in0[:, 3 * wa + ws:].astype(BF16)
    w_kvu = jnp.concatenate([w_k, w_v, w_u], axis=1).astype(BF16)
    w_qvt = jnp.concatenate([w_q, w_v], axis=1).T.astype(BF16)
    w_qkvu = jnp.concatenate([w_q, w_k, w_v, w_u], axis=1).astype(BF16)
    lam_vecs = jnp.concatenate([lambda_q1, lambda_k1, lambda_q2, lambda_k2], axis=0)
    slopes_np = [2.0 ** (-8.0 * (h + 1) / heads) for h in range(heads)]
    slopes = jnp.asarray(slopes_np, F32)
    bd, cd, a_coef = _s5_weights(ssm_a_re[0], ssm_a_im[0], ssm_b_re[0], ssm_b_im[0],
                                 ssm_c_re[0], ssm_c_im[0], ssm_log_dt[0])
    merge_w = (row2(g_mix[0]), w_gate, row2(b_gate[0]), w_glu[0].astype(BF16), row2(b_glu[0]),
               w_proj_a[0].astype(BF16), w_proj_s[0].astype(BF16), w_out[0].astype(BF16),
               row2(g_cross[0]), w_cq[0].astype(BF16))
    ffn_w = (w_co[0].astype(BF16), row2(g_ffn[0]), w_ff_in[0].astype(BF16),
             w_ff_out[0].astype(BF16), row2(g_final))
    xscale = dx ** -0.5

    mkv = _norm_matmul(mem_prompt.reshape(b * n_mem, d), g_mem[0], w_mem_kv[0].astype(BF16), 512)
    mk = mkv[:, :wx].reshape(b, n_mem, wx)
    mv = mkv[:, wx:].reshape(b, n_mem, wx)
    k32, v32, kbf, u_p, qt, vt = _in_proj_prompt(x_prompt, g_mix[0], w_kvu, w_qvt, wa, ws)
    oa_p = _attn_prompt(qt, kbf, vt, lam_vecs, g_subln[0], slopes, heads)
    u_tm = jnp.swapaxes(u_p, 0, 1).reshape(t * b, ws)
    zeros = jnp.zeros((b, g, p), F32)
    y_tm, xf_p = _s5(u_tm, _state_pack(zeros, zeros), bd, cd, a_coef, ssm_d[0], b)
    ys_p = jnp.swapaxes(y_tm.reshape(t, b, ws), 0, 1)
    tm = MERGE_TOKENS
    tok3 = lambda w: pl.BlockSpec((None, tm, w), lambda bi, i: (bi, i, 0))
    x1_p, qx_p = _merge(x_prompt, oa_p, ys_p, tok3(ws), (b, t // tm), tok3, merge_w, xscale, BF16)
    ca_p = _cross(qx_p, mk, mv, heads_x, CROSS_TOKENS)
    y_prompt = _ffn(x1_p.reshape(b * t, d), ca_p.reshape(b * t, wx), ffn_w, FFN_TOKENS).reshape(b, t, d)
    srp, sip = _state_unpack(xf_p, g, p)

    n_s = db * dt_len
    q_s, k_s, v_s, u_s = _in_proj_sample(x_sample.reshape(n_s, d), g_mix[0], w_qkvu, wa, ws)
    oa_s = _attn_sample(q_s.reshape(db, dt_len, wa), k_s.reshape(db, dt_len, wa),
                        v_s.reshape(db, dt_len, wa), cache_k[0], cache_v[0], page_table,
                        lam_vecs, g_subln[0], slopes_np, heads)
    us_tm = jnp.swapaxes(u_s.reshape(db, dt_len, ws), 0, 1).reshape(n_s, ws)
    ys_tm, xf_s = _s5(us_tm, _state_pack(state_ssm_re[0], state_ssm_im[0]), bd, cd, a_coef,
                      ssm_d[0], db)
    ys_s = jnp.swapaxes(ys_tm.reshape(dt_len, db, ws), 0, 1).reshape(n_s, ws)
    tok2 = lambda w: pl.BlockSpec((n_s, w), lambda i: (0, 0))
    x1_s, qx_s = _merge(x_sample.reshape(n_s, d), oa_s.reshape(n_s, wa), ys_s, tok2(ws), (1,),
                        tok2, merge_w, xscale, F32)
    ca_s = _cross(qx_s.reshape(db, dt_len, wx), cache_mem_k[0].reshape(db, n_mem, wx),
                  cache_mem_v[0].reshape(db, n_mem, wx), heads_x, dt_len)
    y_sample = _ffn(x1_s, ca_s.reshape(n_s, wx), ffn_w, FFN_TOKENS).reshape(db, dt_len, d)
    srs, sis = _state_unpack(xf_s, g, p)

    return (y_prompt, y_sample,
            k32.reshape(1, b, t, heads, 2, dk), v32.reshape(1, b, t, heads, dv),
            k_s.reshape(1, db, dt_len, heads, 2, dk), v_s.reshape(1, db, dt_len, heads, dv),
            srp[None], sip[None], srs[None], sis[None],
            mk.reshape(1, b, n_mem, heads_x, dx), mv.reshape(1, b, n_mem, heads_x, dx))
```

```python
import functools
import math

import numpy as np
import jax
import jax.numpy as jnp
from jax import lax
from jax.experimental import pallas as pl
from jax.experimental.pallas import tpu as pltpu

F32 = jnp.float32
BF16 = jnp.bfloat16
EPS = 1e-5
LOG2E = math.log2(math.e)
LAM_INIT = 0.8 - 0.6 * math.exp(-0.3 * 0)

V7X_LANES = 128
V7X_SUBLANES = 8
V7X_VMEM_LIMIT_BYTES = 48 * 1024 * 1024

IN_PROJ_TOKENS = 512
ATTN_Q_BLOCK = 512
SAMPLE_PAGES_PER_STEP = 16
S5_CHUNK_ROWS = 512
MERGE_TOKENS = 512
FFN_TOKENS = 256
CROSS_TOKENS = 512


def _params(*sem):
    return pltpu.CompilerParams(dimension_semantics=sem,
                                vmem_limit_bytes=V7X_VMEM_LIMIT_BYTES)


def _rmsnorm(x, g):
    r = lax.rsqrt(jnp.mean(x * x, axis=-1, keepdims=True) + EPS)
    return (x * r) * g


def _dot(a, b):
    return jnp.dot(a, b, preferred_element_type=F32)


def _dot_nt(a, b):
    return lax.dot_general(a, b, (((1,), (1,)), ((), ())), preferred_element_type=F32)


def _full(shape):
    zeros = (0,) * len(shape)
    return pl.BlockSpec(shape, lambda *_: zeros)


def _norm_matmul_kernel(x_ref, g_ref, w_ref, o_ref):
    h = _rmsnorm(x_ref[...], g_ref[...]).astype(BF16)
    o_ref[...] = _dot(h, w_ref[...])


def _norm_matmul(x, g, w_bf16, tm):
    n, d = x.shape
    n_out = w_bf16.shape[1]
    return pl.pallas_call(
        _norm_matmul_kernel,
        grid=(n // tm,),
        in_specs=[pl.BlockSpec((tm, d), lambda i: (i, 0)), _full((1, d)), _full((d, n_out))],
        out_specs=pl.BlockSpec((tm, n_out), lambda i: (i, 0)),
        out_shape=jax.ShapeDtypeStruct((n, n_out), F32),
        compiler_params=_params("parallel"),
        name="norm_matmul",
    )(x, g.reshape(1, d), w_bf16)


def _in_proj_prompt_kernel(x_ref, g_ref, wkvu_ref, wqvt_ref,
                           k32_ref, v32_ref, kbf_ref, u_ref, qt_ref, vt_ref, *, wa):
    h = _rmsnorm(x_ref[...], g_ref[...]).astype(BF16)
    r = _dot(h, wkvu_ref[...])
    k = r[:, :wa]
    k32_ref[...] = k
    kbf_ref[...] = k.astype(BF16)
    v32_ref[...] = r[:, wa:2 * wa]
    u_ref[...] = r[:, 2 * wa:]
    rt = _dot_nt(wqvt_ref[...], h)
    qt_ref[...] = rt[:wa].astype(BF16)
    vt_ref[...] = rt[wa:].astype(BF16)


def _in_proj_prompt(x, g, wkvu, wqvt, wa, ws):
    b, t, d = x.shape
    tm = IN_PROJ_TOKENS
    row = lambda bi, i: (bi, i, 0)
    col = lambda bi, i: (bi, 0, i)
    return pl.pallas_call(
        functools.partial(_in_proj_prompt_kernel, wa=wa),
        grid=(b, t // tm),
        in_specs=[pl.BlockSpec((None, tm, d), row), _full((1, d)),
                  _full(wkvu.shape), _full(wqvt.shape)],
        out_specs=[pl.BlockSpec((None, tm, wa), row), pl.BlockSpec((None, tm, wa), row),
                   pl.BlockSpec((None, tm, wa), row), pl.BlockSpec((None, tm, ws), row),
                   pl.BlockSpec((None, wa, tm), col), pl.BlockSpec((None, wa, tm), col)],
        out_shape=[jax.ShapeDtypeStruct((b, t, wa), F32), jax.ShapeDtypeStruct((b, t, wa), F32),
                   jax.ShapeDtypeStruct((b, t, wa), BF16), jax.ShapeDtypeStruct((b, t, ws), F32),
                   jax.ShapeDtypeStruct((b, wa, t), BF16), jax.ShapeDtypeStruct((b, wa, t), BF16)],
        compiler_params=_params("parallel", "parallel"),
        name="in_proj_prompt",
    )(x, g.reshape(1, d), wkvu, wqvt)


def _in_proj_sample_kernel(x_ref, g_ref, w_ref, q_ref, k_ref, v_ref, u_ref, *, wa):
    h = _rmsnorm(x_ref[...], g_ref[...]).astype(BF16)
    r = _dot(h, w_ref[...])
    q_ref[...] = r[:, :wa]
    k_ref[...] = r[:, wa:2 * wa]
    v_ref[...] = r[:, 2 * wa:3 * wa]
    u_ref[...] = r[:, 3 * wa:]


def _in_proj_sample(x, g, w, wa, ws):
    n, d = x.shape
    shapes = [jax.ShapeDtypeStruct((n, wa), F32)] * 3 + [jax.ShapeDtypeStruct((n, ws), F32)]
    return pl.pallas_call(
        functools.partial(_in_proj_sample_kernel, wa=wa),
        grid=(1,),
        in_specs=[_full((n, d)), _full((1, d)), _full(w.shape)],
        out_specs=[_full(s.shape) for s in shapes],
        out_shape=shapes,
        compiler_params=_params("arbitrary"),
        name="in_proj_sample",
    )(x, g.reshape(1, d), w)


def _lambda_full(lam_ref):
    lv = lam_ref[...]
    e1 = jnp.exp(jnp.sum(lv[0:1] * lv[1:2], axis=-1, keepdims=True))
    e2 = jnp.exp(jnp.sum(lv[2:3] * lv[3:4], axis=-1, keepdims=True))
    return e1 - e2 + LAM_INIT


def _attn_prompt_kernel(slopes_ref, qt_ref, k_ref, vt_ref, lam_ref, gsub_ref, o_ref,
                        acc_ref, m_ref, l_ref, sa_ref, sb_ref, *, blk, dk):
    i = pl.program_id(2)
    slope = slopes_ref[pl.program_id(1)]
    qt = qt_ref[...]
    zero = jnp.zeros((dk, blk), BF16)
    qm = (jnp.concatenate([qt[:dk], zero], axis=0), jnp.concatenate([zero, qt[dk:]], axis=0))
    krel = lax.broadcasted_iota(jnp.int32, (blk, blk), 0)
    qrel = lax.broadcasted_iota(jnp.int32, (blk, blk), 1)
    kbias = slope * krel.astype(F32)

    def scores(j):
        k = k_ref[pl.ds(pl.multiple_of(j * blk, blk), blk), :]
        return tuple(_dot(k, q) + kbias for q in qm)

    def fold(s_ref, j, masked):
        off = slope * ((j - i) * blk).astype(F32)
        vt = vt_ref[:, pl.ds(pl.multiple_of(j * blk, blk), blk)]
        for mp in range(2):
            s = s_ref[mp]
            if masked:
                s = jnp.where(krel <= qrel, s, -jnp.inf)
            m = m_ref[mp:mp + 1]
            m_new = jnp.maximum(m, jnp.max(s, axis=0, keepdims=True) + off)
            p = jnp.exp2(s - (m_new - off))
            alpha = jnp.exp2(m - m_new)
            acc_ref[mp] = alpha * acc_ref[mp] + _dot(vt, p.astype(BF16))
            l_ref[mp:mp + 1] = alpha * l_ref[mp:mp + 1] + jnp.sum(p, axis=0, keepdims=True)
            m_ref[mp:mp + 1] = m_new

    def stage(j, s_cur_ref, s_next_ref):
        for mp, s in enumerate(scores(j + 1)):
            s_next_ref[mp] = s
        fold(s_cur_ref, j, False)

    def pair(pp, _):
        stage(2 * pp, sa_ref, sb_ref)
        stage(2 * pp + 1, sb_ref, sa_ref)
        return 0

    acc_ref[...] = jnp.zeros_like(acc_ref)
    m_ref[...] = jnp.full_like(m_ref, -jnp.inf)
    l_ref[...] = jnp.zeros_like(l_ref)
    for mp, s in enumerate(scores(0)):
        sa_ref[mp] = s
    lax.fori_loop(0, i // 2, pair, 0)
    odd = i % 2 == 1

    @pl.when(odd)
    def _():
        stage(i - 1, sa_ref, sb_ref)
        fold(sb_ref, i, True)

    @pl.when(jnp.logical_not(odd))
    def _():
        fold(sa_ref, i, True)

    lam = _lambda_full(lam_ref)
    o = acc_ref[0] / l_ref[0:1] - lam * (acc_ref[1] / l_ref[1:2])
    r = lax.rsqrt(jnp.mean(o * o, axis=0, keepdims=True) + EPS)
    o = (o * r) * gsub_ref[...] * (1.0 - LAM_INIT)
    o_ref[...] = o.T.astype(o_ref.dtype)


def _attn_prompt(qt, kbf, vt, lam_vecs, g_subln, slopes, heads):
    b, wa, t = qt.shape
    dv = wa // heads
    dk = dv // 2
    blk = ATTN_Q_BLOCK
    return pl.pallas_call(
        functools.partial(_attn_prompt_kernel, blk=blk, dk=dk),
        grid=(b, heads, t // blk),
        in_specs=[pl.BlockSpec(memory_space=pltpu.SMEM),
                  pl.BlockSpec((None, dv, blk), lambda bi, h, i: (bi, h, i)),
                  pl.BlockSpec((None, t, dv), lambda bi, h, i: (bi, 0, h)),
                  pl.BlockSpec((None, dv, t), lambda bi, h, i: (bi, h, 0)),
                  _full(lam_vecs.shape), _full((dv, 1))],
        out_specs=pl.BlockSpec((None, blk, dv), lambda bi, h, i: (bi, i, h)),
        out_shape=jax.ShapeDtypeStruct((b, t, wa), BF16),
        scratch_shapes=[pltpu.VMEM((2, dv, blk), F32),
                        pltpu.VMEM((2, blk), F32), pltpu.VMEM((2, blk), F32),
                        pltpu.VMEM((2, blk, blk), F32), pltpu.VMEM((2, blk, blk), F32)],
        compiler_params=_params("parallel", "parallel", "arbitrary"),
        name="attn_prompt",
    )(slopes, qt, kbf, vt, lam_vecs, g_subln.reshape(dv, 1))


def _attn_sample_kernel(pt_ref, *refs, pps, heads, tq, past_len, page):
    k_pages = refs[:pps]
    v_pages = refs[pps:2 * pps]
    (q_ref, kn_ref, vn_ref, slope_ref, qidx_ref, qmask_ref, lam_ref, gsub_ref, o_ref,
     qbd_ref, kpad_ref, vpad_ref, m_ref, l_ref, acc_ref) = refs[2 * pps:]
    c = pl.program_id(1)
    hr = 2 * tq
    rows = heads * hr
    dv = q_ref.shape[-1] // heads

    slope = slope_ref[...]
    qidx = qidx_ref[...]
    lane = lax.broadcasted_iota(jnp.int32, (rows, page), 1)

    def fold(s, head_vals):
        m_old = m_ref[...]
        m_new = jnp.maximum(m_old, jnp.max(s, axis=-1, keepdims=True))
        p = jnp.exp(s - m_new)
        alpha = jnp.exp(m_old - m_new)
        l_ref[...] = alpha * l_ref[...] + jnp.sum(p, axis=-1, keepdims=True)
        pb = p.astype(BF16)
        for h in range(heads):
            r = slice(h * hr, (h + 1) * hr)
            pv = _dot(pb[r, :page], head_vals(0, h))
            for j in range(1, s.shape[1] // page):
                pv += _dot(pb[r, j * page:(j + 1) * page], head_vals(j, h))
            acc_ref[r, :] = alpha[r] * acc_ref[r, :] + pv
        m_ref[...] = m_new

    @pl.when(c == 0)
    def _():
        q = q_ref[...]
        qt = jnp.concatenate([q] * (rows // tq), axis=0)
        qbd_ref[...] = (qt * qmask_ref[...]).astype(BF16)
        m_ref[...] = jnp.full_like(m_ref, -jnp.inf)
        l_ref[...] = jnp.zeros_like(l_ref)
        acc_ref[...] = jnp.zeros_like(acc_ref)
        kpad_ref[...] = jnp.zeros_like(kpad_ref)
        vpad_ref[...] = jnp.zeros_like(vpad_ref)
        kpad_ref[0:tq, :] = kn_ref[...]
        vpad_ref[0:tq, :] = vn_ref[...]
        s = _dot_nt(qbd_ref[...], kpad_ref[...].astype(BF16))
        s = s - slope * (qidx - lane).astype(F32)
        s = jnp.where(lane <= qidx, s, -jnp.inf)
        fold(s, lambda j, h: vpad_ref[:, h * dv:(h + 1) * dv].astype(BF16))

    qbd = qbd_ref[...]
    qpos = past_len + qidx
    parts = []
    for j in range(pps):
        s = _dot(qbd, k_pages[j][...].astype(BF16))
        kpos = (c * pps + j) * page + lane
        parts.append(s - slope * (qpos - kpos).astype(F32))
    fold(jnp.concatenate(parts, axis=1),
         lambda j, h: v_pages[j][pl.ds(h, page, stride=heads), :].astype(BF16))

    @pl.when(c == pl.num_programs(1) - 1)
    def _():
        lam = _lambda_full(lam_ref)
        o = acc_ref[...] / l_ref[...]
        for h in range(heads):
            d = o[h * hr:h * hr + tq] - lam * o[h * hr + tq:(h + 1) * hr]
            d = _rmsnorm(d, gsub_ref[...]) * (1.0 - LAM_INIT)
            o_ref[:, h * dv:(h + 1) * dv] = d.astype(o_ref.dtype)


def _attn_sample(q, k_new, v_new, pool_k, pool_v, page_table, lam_vecs, g_subln, slopes_np, heads):
    db, tq, wa = q.shape
    n_phys, page = pool_k.shape[0], pool_k.shape[1]
    n_pages = page_table.shape[1]
    past_len = n_pages * page
    pps = math.gcd(SAMPLE_PAGES_PER_STEP, n_pages)
    dv = wa // heads
    dk = dv // 2
    rows = heads * 2 * tq
    pk = jnp.transpose(pool_k, (0, 2, 3, 4, 1)).reshape(n_phys, wa, page)
    pv = pool_v.reshape(n_phys, page * heads, dv)
    r_idx = np.arange(rows)
    slope_rows = jnp.asarray(np.repeat(np.asarray(slopes_np)[r_idx // (2 * tq)][:, None], page, 1), F32)
    qidx_rows = jnp.asarray(np.repeat((r_idx % tq)[:, None], page, 1), jnp.int32)
    qmask = jnp.asarray((r_idx // tq)[:, None] == (np.arange(wa) // dk)[None, :], F32)

    def page_spec(j, shape):
        return pl.BlockSpec((None,) + shape,
                            lambda b, c, pt: (pt[b * n_pages + c * pps + j], 0, 0))

    per_b = lambda b, c, pt: (b, 0, 0)
    const2 = lambda b, c, pt: (0, 0)
    grid_spec = pltpu.PrefetchScalarGridSpec(
        num_scalar_prefetch=1,
        grid=(db, n_pages // pps),
        in_specs=([page_spec(j, (wa, page)) for j in range(pps)]
                  + [page_spec(j, (page * heads, dv)) for j in range(pps)]
                  + [pl.BlockSpec((None, tq, wa), per_b)] * 3
                  + [pl.BlockSpec((rows, page), const2), pl.BlockSpec((rows, page), const2),
                     pl.BlockSpec((rows, wa), const2),
                     pl.BlockSpec(lam_vecs.shape, const2),
                     pl.BlockSpec((1, dv), const2)]),
        out_specs=pl.BlockSpec((None, tq, wa), per_b),
        scratch_shapes=[pltpu.VMEM((rows, wa), BF16),
                        pltpu.VMEM((page, wa), F32), pltpu.VMEM((page, wa), F32),
                        pltpu.VMEM((rows, 1), F32), pltpu.VMEM((rows, 1), F32),
                        pltpu.VMEM((rows, dv), F32)])
    return pl.pallas_call(
        functools.partial(_attn_sample_kernel, pps=pps, heads=heads, tq=tq,
                          past_len=past_len, page=page),
        grid_spec=grid_spec,
        out_shape=jax.ShapeDtypeStruct((db, tq, wa), F32),
        compiler_params=_params("parallel", "arbitrary"),
        name="attn_sample",
    )(page_table.reshape(-1), *([pk] * pps), *([pv] * pps), q, k_new, v_new,
      slope_rows, qidx_rows, qmask, lam_vecs, g_subln.reshape(1, dv))


def _s5_kernel(u_ref, x0_ref, bd_ref, cd_ref, a_ref, d_ref, y_ref, xf_ref, xs_ref, st_ref,
               *, nb, steps, half_in, half_state):
    c = pl.program_id(0)

    @pl.when(c == 0)
    def _():
        st_ref[...] = x0_ref[...]

    u = u_ref[...]
    ub = u.astype(BF16)
    hs = half_state
    for k in range(2):
        xs_ref[:, 2 * k * hs:2 * (k + 1) * hs] = _dot(
            ub[:, k * half_in:(k + 1) * half_in], bd_ref[k])

    sub = V7X_SUBLANES
    for k in range(2):
        re0, im0 = 2 * k * hs, (2 * k + 1) * hs
        a_re = a_ref[2 * k]
        a_im = a_ref[2 * k + 1]

        def group(r, _):
            rows0 = pl.multiple_of(r * sub, sub)
            x_re = st_ref[pl.ds(rows0, sub), re0:re0 + hs]
            x_im = st_ref[pl.ds(rows0, sub), im0:im0 + hs]

            def step(t, carry):
                x_re, x_im = carry
                row = pl.multiple_of(t * nb + rows0, sub)
                n_re = a_re * x_re - a_im * x_im + xs_ref[pl.ds(row, sub), re0:re0 + hs]
                n_im = a_re * x_im + a_im * x_re + xs_ref[pl.ds(row, sub), im0:im0 + hs]
                xs_ref[pl.ds(row, sub), re0:re0 + hs] = n_re
                xs_ref[pl.ds(row, sub), im0:im0 + hs] = n_im
                return n_re, n_im

            x_re, x_im = lax.fori_loop(0, steps, step, (x_re, x_im))
            st_ref[pl.ds(rows0, sub), re0:re0 + hs] = x_re
            st_ref[pl.ds(rows0, sub), im0:im0 + hs] = x_im
            return 0

        lax.fori_loop(0, nb // sub, group, 0)

    ys = [_dot(xs_ref[:, 2 * k * hs:2 * (k + 1) * hs].astype(BF16), cd_ref[k]) for k in range(2)]
    y_ref[...] = jnp.concatenate(ys, axis=1) + d_ref[...] * u

    @pl.when(c == pl.num_programs(0) - 1)
    def _():
        xf_ref[...] = st_ref[...]


def _s5(u_tm, x0, bd, cd, a_coef, d_vec, nb):
    rows_total, width = u_tm.shape
    steps = min(rows_total // nb, max(1, S5_CHUNK_ROWS // nb))
    rows = steps * nb
    n_state = x0.shape[1]
    return pl.pallas_call(
        functools.partial(_s5_kernel, nb=nb, steps=steps, half_in=width // 2,
                          half_state=n_state // 4),
        grid=(rows_total // rows,),
        in_specs=[pl.BlockSpec((rows, width), lambda c: (c, 0)), _full(x0.shape),
                  _full(bd.shape), _full(cd.shape), _full(a_coef.shape), _full((1, width))],
        out_specs=[pl.BlockSpec((rows, width), lambda c: (c, 0)), _full(x0.shape)],
        out_shape=[jax.ShapeDtypeStruct((rows_total, width), F32),
                   jax.ShapeDtypeStruct(x0.shape, F32)],
        scratch_shapes=[pltpu.VMEM((rows, n_state), F32), pltpu.VMEM(x0.shape, F32)],
        compiler_params=_params("arbitrary"),
        name="s5_scan",
    )(u_tm, x0, bd, cd, a_coef, d_vec.reshape(1, width))


def _s5_weights(a_re, a_im, b_re, b_im, c_re, c_im, log_dt):
    g, p, ch = b_re.shape
    dt = jnp.exp(log_dt)[:, None]
    mag = jnp.exp(dt * a_re)
    abar_re, abar_im = mag * jnp.cos(dt * a_im), mag * jnp.sin(dt * a_im)
    den = a_re * a_re + a_im * a_im
    f_re = ((abar_re - 1.0) * a_re + abar_im * a_im) / den
    f_im = (abar_im * a_re - (abar_re - 1.0) * a_im) / den
    bbar_re = f_re[..., None] * b_re - f_im[..., None] * b_im
    bbar_im = f_re[..., None] * b_im + f_im[..., None] * b_re
    gh = g // 2
    eye = jnp.eye(gh, dtype=F32)

    def bd_half(w):
        return jnp.einsum('gpc,gh->gchp', w, eye).reshape(gh * ch, gh * p)

    def cd_half(w):
        return jnp.einsum('gcp,gh->gphc', w, eye).reshape(gh * p, gh * ch)

    bd = jnp.stack([jnp.concatenate([bd_half(bbar_re[k * gh:(k + 1) * gh]),
                                     bd_half(bbar_im[k * gh:(k + 1) * gh])], axis=1)
                    for k in range(2)]).astype(BF16)
    cd = jnp.stack([jnp.concatenate([cd_half(c_re[k * gh:(k + 1) * gh]),
                                     cd_half(-c_im[k * gh:(k + 1) * gh])], axis=0)
                    for k in range(2)]).astype(BF16)
    coef = [w[k * gh:(k + 1) * gh].reshape(1, gh * p) for k in range(2) for w in (abar_re, abar_im)]
    a_coef = jnp.broadcast_to(jnp.stack(coef), (4, V7X_SUBLANES, gh * p))
    return bd, cd, a_coef


def _state_pack(x_re, x_im):
    nb, g, p = x_re.shape
    hs = g * p // 2
    re = x_re.reshape(nb, 2, hs)
    im = x_im.reshape(nb, 2, hs)
    return jnp.stack([re[:, 0], im[:, 0], re[:, 1], im[:, 1]], axis=1).reshape(nb, 4 * hs)


def _state_unpack(x, g, p):
    nb = x.shape[0]
    hs = g * p // 2
    x = x.reshape(nb, 4, hs)
    re = jnp.stack([x[:, 0], x[:, 2]], axis=1).reshape(nb, g, p)
    im = jnp.stack([x[:, 1], x[:, 3]], axis=1).reshape(nb, g, p)
    return re, im


def _merge_kernel(x_ref, oa_ref, ys_ref, gmix_ref, wg_ref, bg_ref, wglu_ref, bglu_ref,
                  wpa_ref, wps_ref, wout_ref, gcross_ref, wcq_ref, x1_ref, qx_ref, *, qscale):
    x = x_ref[...]
    d = x.shape[-1]
    h = _rmsnorm(x, gmix_ref[...]).astype(BF16)
    gate = jax.nn.sigmoid(_dot(h, wg_ref[...]) + bg_ref[...])
    z = jax.nn.gelu(ys_ref[...])
    o_s = z * jax.nn.sigmoid(_dot(z.astype(BF16), wglu_ref[...]) + bglu_ref[...])
    pa = _dot(oa_ref[...].astype(BF16), wpa_ref[...])
    ps = _dot(o_s.astype(BF16), wps_ref[...])
    mix = gate[:, :d] * pa + gate[:, d:] * ps
    x1 = x + _dot(mix.astype(BF16), wout_ref[...])
    x1_ref[...] = x1
    hc = _rmsnorm(x1, gcross_ref[...]).astype(BF16)
    qx_ref[...] = (_dot(hc, wcq_ref[...]) * qscale).astype(qx_ref.dtype)


def _merge(x, oa, ys, ys_spec, grid, tok_spec, weights, qscale, inter_dtype):
    (gmix, wg, bg, wglu, bglu, wpa, wps, wout, gcross, wcq) = weights
    d = x.shape[-1]
    wx = wcq.shape[1]
    wspecs = [_full(w.shape) for w in weights]
    return pl.pallas_call(
        functools.partial(_merge_kernel, qscale=qscale),
        grid=grid,
        in_specs=[tok_spec(d), tok_spec(oa.shape[-1]), ys_spec] + wspecs,
        out_specs=[tok_spec(d), tok_spec(wx)],
        out_shape=[jax.ShapeDtypeStruct(x.shape, F32),
                   jax.ShapeDtypeStruct(x.shape[:-1] + (wx,), inter_dtype)],
        compiler_params=_params(*(("parallel",) * len(grid))),
        name="merge",
    )(x, oa, ys, *weights)


def _cross_kernel(q_ref, k_ref, v_ref, o_ref, *, heads):
    dx = q_ref.shape[-1] // heads
    for h in range(heads):
        sl = slice(h * dx, (h + 1) * dx)
        q = q_ref[:, sl].astype(BF16)
        s = _dot_nt(q, k_ref[:, sl].astype(BF16))
        m = jnp.max(s, axis=-1, keepdims=True)
        p = jnp.exp(s - m)
        l = jnp.sum(p, axis=-1, keepdims=True)
        o = _dot(p.astype(BF16), v_ref[:, sl].astype(BF16)) / l
        o_ref[:, sl] = o.astype(o_ref.dtype)


def _cross(qx, mem_k, mem_v, heads, tm):
    b, t, wx = qx.shape
    n_mem = mem_k.shape[1]
    tok = pl.BlockSpec((None, tm, wx), lambda bi, i: (bi, i, 0))
    mem = pl.BlockSpec((None, n_mem, wx), lambda bi, i: (bi, 0, 0))
    return pl.pallas_call(
        functools.partial(_cross_kernel, heads=heads),
        grid=(b, t // tm),
        in_specs=[tok, mem, mem],
        out_specs=tok,
        out_shape=jax.ShapeDtypeStruct(qx.shape, qx.dtype),
        compiler_params=_params("parallel", "parallel"),
        name="cross_attn",
    )(qx, mem_k, mem_v)


def _ffn_kernel(x_ref, ca_ref, wco_ref, gffn_ref, wfi_ref, wfo_ref, gfin_ref, y_ref, *, dff):
    x2 = x_ref[...] + _dot(ca_ref[...].astype(BF16), wco_ref[...])
    h = _rmsnorm(x2, gffn_ref[...]).astype(BF16)
    ff = _dot(h, wfi_ref[...])
    hid = jax.nn.silu(ff[:, :dff]) * ff[:, dff:]
    x3 = x2 + _dot(hid.astype(BF16), wfo_ref[...])
    y_ref[...] = _rmsnorm(x3, gfin_ref[...])


def _ffn(x1, ca, weights, tm):
    (wco, gffn, wfi, wfo, gfin) = weights
    n, d = x1.shape
    tm = min(tm, n)
    tok = lambda w: pl.BlockSpec((tm, w), lambda i: (i, 0))
    return pl.pallas_call(
        functools.partial(_ffn_kernel, dff=wfo.shape[0]),
        grid=(n // tm,),
        in_specs=[tok(d), tok(ca.shape[-1])] + [_full(w.shape) for w in weights],
        out_specs=tok(d),
        out_shape=jax.ShapeDtypeStruct((n, d), F32),
        compiler_params=_params("parallel"),
        name="ffn",
    )(x1, ca, *weights)


def kernel(x_prompt, x_sample, mem_prompt, cache_k, cache_v, state_ssm_re, state_ssm_im, cache_mem_k, cache_mem_v, page_table, g_mix, w_in, b_gate, lambda_q1, lambda_k1, lambda_q2, lambda_k2, g_subln, ssm_a_re, ssm_a_im, ssm_b_re, ssm_b_im, ssm_c_re, ssm_c_im, ssm_d, ssm_log_dt, w_glu, b_glu, w_proj_a, w_proj_s, w_out, g_cross, g_mem, w_cq, w_mem_kv, w_co, g_ffn, w_ff_in, w_ff_out, g_final):
    assert g_mix.shape[0] == 1, "single-layer trunk"
    b, t, d = x_prompt.shape
    db, dt_len, _ = x_sample.shape
    heads = cache_k.shape[3]
    dk = cache_k.shape[5]
    dv = cache_v.shape[4]
    wa = heads * dv
    g, p, _ = ssm_b_re.shape[1:]
    ws = ssm_d.shape[1]
    n_mem, heads_x, dx = cache_mem_k.shape[2:]
    wx = heads_x * dx
    row2 = lambda v: v.reshape(1, -1)

    w_in0 = w_in[0]
    qs = dk ** -0.5
    w_q, w_k, w_v, w_u = (w_in0[:, :wa] * qs, w_in0[:, wa:2 * wa],
                          w_in0[:, 2 * wa:3 * wa], w_in0[:, 3 * wa:3 * wa + ws])
    w_gate = w_in0[:, 3 * wa + ws:].astype(BF16)
    w_kvu = jnp.concatenate([w_k, w_v, w_u], axis=1).astype(BF16)
    w_qvt = jnp.concatenate([w_q * LOG2E, w_v], axis=1).T.astype(BF16)
    w_qkvu = jnp.concatenate([w_q, w_k, w_v, w_u], axis=1).astype(BF16)
    lam_vecs = jnp.concatenate([lambda_q1, lambda_k1, lambda_q2, lambda_k2], axis=0)
    slopes_np = [2.0 ** (-8.0 * (h + 1) / heads) for h in range(heads)]
    slopes = jnp.asarray(slopes_np, F32) * LOG2E
    bd, cd, a_coef = _s5_weights(ssm_a_re[0], ssm_a_im[0], ssm_b_re[0], ssm_b_im[0],
                                 ssm_c_re[0], ssm_c_im[0], ssm_log_dt[0])
    merge_w = (row2(g_mix[0]), w_gate, row2(b_gate[0]), w_glu[0].astype(BF16), row2(b_glu[0]),
               w_proj_a[0].astype(BF16), w_proj_s[0].astype(BF16), w_out[0].astype(BF16),
               row2(g_cross[0]), w_cq[0].astype(BF16))
    ffn_w = (w_co[0].astype(BF16), row2(g_ffn[0]), w_ff_in[0].astype(BF16),
             w_ff_out[0].astype(BF16), row2(g_final))
    xscale = dx ** -0.5

    mkv = _norm_matmul(mem_prompt.reshape(b * n_mem, d), g_mem[0], w_mem_kv[0].astype(BF16), 512)
    mk = mkv[:, :wx].reshape(b, n_mem, wx)
    mv = mkv[:, wx:].reshape(b, n_mem, wx)
    k32, v32, kbf, u_p, qt, vt = _in_proj_prompt(x_prompt, g_mix[0], w_kvu, w_qvt, wa, ws)
    oa_p = _attn_prompt(qt, kbf, vt, lam_vecs, g_subln[0], slopes, heads)
    u_tm = jnp.swapaxes(u_p, 0, 1).reshape(t * b, ws)
    zeros = jnp.zeros((b, g, p), F32)
    y_tm, xf_p = _s5(u_tm, _state_pack(zeros, zeros), bd, cd, a_coef, ssm_d[0], b)
    ys_p = jnp.swapaxes(y_tm.reshape(t, b, ws), 0, 1)
    tm = MERGE_TOKENS
    tok3 = lambda w: pl.BlockSpec((None, tm, w), lambda bi, i: (bi, i, 0))
    x1_p, qx_p = _merge(x_prompt, oa_p, ys_p, tok3(ws), (b, t // tm), tok3, merge_w, xscale, BF16)
    ca_p = _cross(qx_p, mk, mv, heads_x, CROSS_TOKENS)
    y_prompt = _ffn(x1_p.reshape(b * t, d), ca_p.reshape(b * t, wx), ffn_w, FFN_TOKENS).reshape(b, t, d)
    srp, sip = _state_unpack(xf_p, g, p)

    n_s = db * dt_len
    q_s, k_s, v_s, u_s = _in_proj_sample(x_sample.reshape(n_s, d), g_mix[0], w_qkvu, wa, ws)
    oa_s = _attn_sample(q_s.reshape(db, dt_len, wa), k_s.reshape(db, dt_len, wa),
                        v_s.reshape(db, dt_len, wa), cache_k[0], cache_v[0], page_table,
                        lam_vecs, g_subln[0], slopes_np, heads)
    us_tm = jnp.swapaxes(u_s.reshape(db, dt_len, ws), 0, 1).reshape(n_s, ws)
    ys_tm, xf_s = _s5(us_tm, _state_pack(state_ssm_re[0], state_ssm_im[0]), bd, cd, a_coef,
                      ssm_d[0], db)
    ys_s = jnp.swapaxes(ys_tm.reshape(dt_len, db, ws), 0, 1).reshape(n_s, ws)
    tok2 = lambda w: pl.BlockSpec((n_s, w), lambda i: (0, 0))
    x1_s, qx_s = _merge(x_sample.reshape(n_s, d), oa_s.reshape(n_s, wa), ys_s, tok2(ws), (1,),
                        tok2, merge_w, xscale, F32)
    ca_s = _cross(qx_s.reshape(db, dt_len, wx), cache_mem_k[0].reshape(db, n_mem, wx),
                  cache_mem_v[0].reshape(db, n_mem, wx), heads_x, dt_len)
    y_sample = _ffn(x1_s, ca_s.reshape(n_s, wx), ffn_w, FFN_TOKENS).reshape(db, dt_len, d)
    srs, sis = _state_unpack(xf_s, g, p)

    return (y_prompt, y_sample,
            k32.reshape(1, b, t, heads, 2, dk), v32.reshape(1, b, t, heads, dv),
            k_s.reshape(1, db, dt_len, heads, 2, dk), v_s.reshape(1, db, dt_len, heads, dv),
            srp[None], sip[None], srs[None], sis[None],
            mk.reshape(1, b, n_mem, heads_x, dx), mv.reshape(1, b, n_mem, heads_x, dx))
```

```python
import functools
import math

import numpy as np
import jax
import jax.numpy as jnp
from jax import lax
from jax.experimental import pallas as pl
from jax.experimental.pallas import tpu as pltpu

F32 = jnp.float32
BF16 = jnp.bfloat16
EPS = 1e-5
LOG2E = math.log2(math.e)
LAM_INIT = 0.8 - 0.6 * math.exp(-0.3 * 0)

V7X_LANES = 128
V7X_SUBLANES = 8
V7X_VMEM_LIMIT_BYTES = 48 * 1024 * 1024

IN_PROJ_TOKENS = 512
ATTN_Q_BLOCK = 512
ATTN_UNROLL = 4
SAMPLE_PAGES_PER_STEP = 16
S5_CHUNK_ROWS = 512
S5_BATCH_BLOCK = 64
MERGE_TOKENS = 512
FFN_TOKENS = 256
CROSS_TOKENS = 512


def _params(*sem):
    return pltpu.CompilerParams(dimension_semantics=sem,
                                vmem_limit_bytes=V7X_VMEM_LIMIT_BYTES)


def _rmsnorm(x, g):
    r = lax.rsqrt(jnp.mean(x * x, axis=-1, keepdims=True) + EPS)
    return (x * r) * g


def _dot(a, b):
    return jnp.dot(a, b, preferred_element_type=F32)


def _dot_nt(a, b):
    return lax.dot_general(a, b, (((1,), (1,)), ((), ())), preferred_element_type=F32)


def _full(shape):
    zeros = (0,) * len(shape)
    return pl.BlockSpec(shape, lambda *_: zeros)


def _norm_matmul_kernel(x_ref, g_ref, w_ref, o_ref):
    h = _rmsnorm(x_ref[...], g_ref[...]).astype(BF16)
    o_ref[...] = _dot(h, w_ref[...])


def _norm_matmul(x, g, w_bf16, tm):
    n, d = x.shape
    n_out = w_bf16.shape[1]
    return pl.pallas_call(
        _norm_matmul_kernel,
        grid=(n // tm,),
        in_specs=[pl.BlockSpec((tm, d), lambda i: (i, 0)), _full((1, d)), _full((d, n_out))],
        out_specs=pl.BlockSpec((tm, n_out), lambda i: (i, 0)),
        out_shape=jax.ShapeDtypeStruct((n, n_out), F32),
        compiler_params=_params("parallel"),
        name="norm_matmul",
    )(x, g.reshape(1, d), w_bf16)


def _in_proj_prompt_kernel(x_ref, g_ref, wkvu_ref, wqvt_ref,
                           k32_ref, v32_ref, kbf_ref, u_ref, qt_ref, vt_ref, *, wa):
    h = _rmsnorm(x_ref[...], g_ref[...]).astype(BF16)
    r = _dot(h, wkvu_ref[...])
    k = r[:, :wa]
    k32_ref[...] = k
    kbf_ref[...] = k.astype(BF16)
    v = r[:, wa:2 * wa]
    v32_ref[...] = v
    vt_ref[...] = v.T.astype(BF16)
    u_ref[...] = r[:, 2 * wa:].reshape(u_ref.shape)
    qt_ref[...] = _dot_nt(wqvt_ref[...], h).astype(BF16)


def _in_proj_prompt(x, g, wkvu, wqvt, wa, ws):
    b, t, d = x.shape
    tm = IN_PROJ_TOKENS
    sub = V7X_SUBLANES
    row = lambda bi, i: (bi, i, 0)
    col = lambda bi, i: (bi, 0, i)
    return pl.pallas_call(
        functools.partial(_in_proj_prompt_kernel, wa=wa),
        grid=(b, t // tm),
        in_specs=[pl.BlockSpec((None, tm, d), row), _full((1, d)),
                  _full(wkvu.shape), _full(wqvt.shape)],
        out_specs=[pl.BlockSpec((None, tm, wa), row), pl.BlockSpec((None, tm, wa), row),
                   pl.BlockSpec((None, tm, wa), row),
                   pl.BlockSpec((tm // sub, None, sub, ws), lambda bi, i: (i, bi, 0, 0)),
                   pl.BlockSpec((None, wa, tm), col), pl.BlockSpec((None, wa, tm), col)],
        out_shape=[jax.ShapeDtypeStruct((b, t, wa), F32), jax.ShapeDtypeStruct((b, t, wa), F32),
                   jax.ShapeDtypeStruct((b, t, wa), BF16),
                   jax.ShapeDtypeStruct((t // sub, b, sub, ws), F32),
                   jax.ShapeDtypeStruct((b, wa, t), BF16), jax.ShapeDtypeStruct((b, wa, t), BF16)],
        compiler_params=_params("parallel", "parallel"),
        name="in_proj_prompt",
    )(x, g.reshape(1, d), wkvu, wqvt)


def _in_proj_sample_kernel(x_ref, g_ref, w_ref, q_ref, k_ref, v_ref, u_ref, *, wa):
    h = _rmsnorm(x_ref[...], g_ref[...]).astype(BF16)
    r = _dot(h, w_ref[...])
    q_ref[...] = r[:, :wa]
    k_ref[...] = r[:, wa:2 * wa]
    v_ref[...] = r[:, 2 * wa:3 * wa]
    u_ref[...] = r[:, 3 * wa:]


def _in_proj_sample(x, g, w, wa, ws):
    n, d = x.shape
    shapes = [jax.ShapeDtypeStruct((n, wa), F32)] * 3 + [jax.ShapeDtypeStruct((n, ws), F32)]
    return pl.pallas_call(
        functools.partial(_in_proj_sample_kernel, wa=wa),
        grid=(1,),
        in_specs=[_full((n, d)), _full((1, d)), _full(w.shape)],
        out_specs=[_full(s.shape) for s in shapes],
        out_shape=shapes,
        compiler_params=_params("arbitrary"),
        name="in_proj_sample",
    )(x, g.reshape(1, d), w)


def _lambda_full(lam_ref):
    lv = lam_ref[...]
    e1 = jnp.exp(jnp.sum(lv[0:1] * lv[1:2], axis=-1, keepdims=True))
    e2 = jnp.exp(jnp.sum(lv[2:3] * lv[3:4], axis=-1, keepdims=True))
    return e1 - e2 + LAM_INIT


def _attn_prompt_kernel(slopes_ref, qt_ref, k_ref, vt_ref, lam_ref, gsub_ref, o_ref,
                        acc_ref, m_ref, sa_ref, sb_ref, *, blk, dk, ones_rows, unroll):
    i = pl.program_id(2)
    slope = slopes_ref[pl.program_id(1)]
    dv = vt_ref.shape[0]
    qt = qt_ref[...]
    zero = jnp.zeros((dk, blk), BF16)
    qm = (jnp.concatenate([qt[:dk], zero], axis=0), jnp.concatenate([zero, qt[dk:]], axis=0))
    v_ones = jnp.ones((ones_rows, blk), BF16)
    krel = lax.broadcasted_iota(jnp.int32, (blk, blk), 0)
    qrel = lax.broadcasted_iota(jnp.int32, (blk, blk), 1)
    kbias = slope * krel.astype(F32)

    def scores(j):
        k = k_ref[pl.ds(pl.multiple_of(j * blk, blk), blk), :]
        return tuple(_dot(k, q) + kbias for q in qm)

    def fold(s_ref, j, masked):
        off = slope * ((j - i) * blk).astype(F32)
        vt = vt_ref[:, pl.ds(pl.multiple_of(j * blk, blk), blk)]
        vta = jnp.concatenate([vt, v_ones], axis=0)
        for mp in range(2):
            s = s_ref[mp]
            if masked:
                s = jnp.where(krel <= qrel, s, -jnp.inf)
            m = m_ref[mp:mp + 1]
            m_new = jnp.maximum(m, jnp.max(s, axis=0, keepdims=True) + off)
            p = jnp.exp2(s - (m_new - off)).astype(BF16)
            acc_ref[mp] = jnp.exp2(m - m_new) * acc_ref[mp] + _dot(vta, p)
            m_ref[mp:mp + 1] = m_new

    def stage(j, s_cur_ref, s_next_ref):
        for mp, s in enumerate(scores(j + 1)):
            s_next_ref[mp] = s
        fold(s_cur_ref, j, False)

    s_bufs = (sa_ref, sb_ref)

    def unrolled(n, _):
        for u in range(unroll):
            stage(unroll * n + u, s_bufs[u % 2], s_bufs[(u + 1) % 2])
        return 0

    acc_ref[...] = jnp.zeros_like(acc_ref)
    m_ref[...] = jnp.full_like(m_ref, -jnp.inf)
    for mp, s in enumerate(scores(0)):
        sa_ref[mp] = s
    n_full = i // unroll
    lax.fori_loop(0, n_full, unrolled, 0)
    rest = i - n_full * unroll
    for u in range(unroll - 1):
        @pl.when(rest > u)
        def _():
            stage(n_full * unroll + u, s_bufs[u % 2], s_bufs[(u + 1) % 2])

    for parity in range(2):
        @pl.when(rest % 2 == parity)
        def _():
            fold(s_bufs[parity], i, True)

    lam = _lambda_full(lam_ref)
    a0 = acc_ref[0]
    a1 = acc_ref[1]
    o = a0[:dv] / a0[dv:dv + 1] - lam * (a1[:dv] / a1[dv:dv + 1])
    r = lax.rsqrt(jnp.mean(o * o, axis=0, keepdims=True) + EPS)
    o = (o * r) * gsub_ref[...] * (1.0 - LAM_INIT)
    o_ref[...] = o.T.astype(o_ref.dtype)


def _attn_prompt(qt, kbf, vt, lam_vecs, g_subln, slopes_np, heads):
    b, wa, t = qt.shape
    dv = wa // heads
    dk = dv // 2
    blk = ATTN_Q_BLOCK
    ones_rows = 2 * V7X_SUBLANES
    slopes = jnp.asarray(slopes_np, F32) * LOG2E
    return pl.pallas_call(
        functools.partial(_attn_prompt_kernel, blk=blk, dk=dk, ones_rows=ones_rows,
                          unroll=ATTN_UNROLL),
        grid=(b, heads, t // blk),
        in_specs=[pl.BlockSpec(memory_space=pltpu.SMEM),
                  pl.BlockSpec((None, dv, blk), lambda bi, h, i: (bi, h, i)),
                  pl.BlockSpec((None, t, dv), lambda bi, h, i: (bi, 0, h)),
                  pl.BlockSpec((None, dv, t), lambda bi, h, i: (bi, h, 0)),
                  _full(lam_vecs.shape), _full((dv, 1))],
        out_specs=pl.BlockSpec((None, blk, dv), lambda bi, h, i: (bi, i, h)),
        out_shape=jax.ShapeDtypeStruct((b, t, wa), BF16),
        scratch_shapes=[pltpu.VMEM((2, dv + ones_rows, blk), F32), pltpu.VMEM((2, blk), F32),
                        pltpu.VMEM((2, blk, blk), F32), pltpu.VMEM((2, blk, blk), F32)],
        compiler_params=_params("parallel", "parallel", "arbitrary"),
        name="attn_prompt",
    )(slopes, qt, kbf, vt, lam_vecs, g_subln.reshape(dv, 1))


def _attn_sample_kernel(pt_ref, *refs, pps, heads, tq, past_len, page):
    k_pages = refs[:pps]
    v_pages = refs[pps:2 * pps]
    (q_ref, kn_ref, vn_ref, slope_ref, qidx_ref, qmask_ref, lam_ref, gsub_ref, o_ref,
     qbd_ref, kpad_ref, vpad_ref, m_ref, l_ref, acc_ref) = refs[2 * pps:]
    c = pl.program_id(1)
    hr = 2 * tq
    rows = heads * hr
    dv = q_ref.shape[-1] // heads

    slope = slope_ref[...]
    qidx = qidx_ref[...]
    lane = lax.broadcasted_iota(jnp.int32, (rows, page), 1)

    def fold(s, head_vals):
        m_old = m_ref[...]
        m_new = jnp.maximum(m_old, jnp.max(s, axis=-1, keepdims=True))
        p = jnp.exp(s - m_new)
        alpha = jnp.exp(m_old - m_new)
        l_ref[...] = alpha * l_ref[...] + jnp.sum(p, axis=-1, keepdims=True)
        pb = p.astype(BF16)
        for h in range(heads):
            r = slice(h * hr, (h + 1) * hr)
            pv = _dot(pb[r, :page], head_vals(0, h))
            for j in range(1, s.shape[1] // page):
                pv += _dot(pb[r, j * page:(j + 1) * page], head_vals(j, h))
            acc_ref[r, :] = alpha[r] * acc_ref[r, :] + pv
        m_ref[...] = m_new

    @pl.when(c == 0)
    def _():
        q = q_ref[...]
        qt = jnp.concatenate([q] * (rows // tq), axis=0)
        qbd_ref[...] = (qt * qmask_ref[...]).astype(BF16)
        m_ref[...] = jnp.full_like(m_ref, -jnp.inf)
        l_ref[...] = jnp.zeros_like(l_ref)
        acc_ref[...] = jnp.zeros_like(acc_ref)
        kpad_ref[...] = jnp.zeros_like(kpad_ref)
        vpad_ref[...] = jnp.zeros_like(vpad_ref)
        kpad_ref[0:tq, :] = kn_ref[...]
        vpad_ref[0:tq, :] = vn_ref[...]
        s = _dot_nt(qbd_ref[...], kpad_ref[...].astype(BF16))
        s = s - slope * (qidx - lane).astype(F32)
        s = jnp.where(lane <= qidx, s, -jnp.inf)
        fold(s, lambda j, h: vpad_ref[:, h * dv:(h + 1) * dv].astype(BF16))

    qbd = qbd_ref[...]
    qpos = past_len + qidx
    parts = []
    for j in range(pps):
        s = _dot(qbd, k_pages[j][...].astype(BF16))
        kpos = (c * pps + j) * page + lane
        parts.append(s - slope * (qpos - kpos).astype(F32))
    fold(jnp.concatenate(parts, axis=1),
         lambda j, h: v_pages[j][pl.ds(h, page, stride=heads), :].astype(BF16))

    @pl.when(c == pl.num_programs(1) - 1)
    def _():
        lam = _lambda_full(lam_ref)
        o = acc_ref[...] / l_ref[...]
        for h in range(heads):
            d = o[h * hr:h * hr + tq] - lam * o[h * hr + tq:(h + 1) * hr]
            d = _rmsnorm(d, gsub_ref[...]) * (1.0 - LAM_INIT)
            o_ref[:, h * dv:(h + 1) * dv] = d.astype(o_ref.dtype)


def _attn_sample(q, k_new, v_new, pool_k, pool_v, page_table, lam_vecs, g_subln, slopes_np, heads):
    db, tq, wa = q.shape
    n_phys, page = pool_k.shape[0], pool_k.shape[1]
    n_pages = page_table.shape[1]
    past_len = n_pages * page
    pps = math.gcd(SAMPLE_PAGES_PER_STEP, n_pages)
    dv = wa // heads
    dk = dv // 2
    rows = heads * 2 * tq
    pk = jnp.transpose(pool_k, (0, 2, 3, 4, 1)).reshape(n_phys, wa, page)
    pv = pool_v.reshape(n_phys, page * heads, dv)
    r_idx = np.arange(rows)
    slope_rows = jnp.asarray(np.repeat(np.asarray(slopes_np)[r_idx // (2 * tq)][:, None], page, 1), F32)
    qidx_rows = jnp.asarray(np.repeat((r_idx % tq)[:, None], page, 1), jnp.int32)
    qmask = jnp.asarray((r_idx // tq)[:, None] == (np.arange(wa) // dk)[None, :], F32)

    def page_spec(j, shape):
        return pl.BlockSpec((None,) + shape,
                            lambda b, c, pt: (pt[b * n_pages + c * pps + j], 0, 0))

    per_b = lambda b, c, pt: (b, 0, 0)
    const2 = lambda b, c, pt: (0, 0)
    grid_spec = pltpu.PrefetchScalarGridSpec(
        num_scalar_prefetch=1,
        grid=(db, n_pages // pps),
        in_specs=([page_spec(j, (wa, page)) for j in range(pps)]
                  + [page_spec(j, (page * heads, dv)) for j in range(pps)]
                  + [pl.BlockSpec((None, tq, wa), per_b)] * 3
                  + [pl.BlockSpec((rows, page), const2), pl.BlockSpec((rows, page), const2),
                     pl.BlockSpec((rows, wa), const2),
                     pl.BlockSpec(lam_vecs.shape, const2),
                     pl.BlockSpec((1, dv), const2)]),
        out_specs=pl.BlockSpec((None, tq, wa), per_b),
        scratch_shapes=[pltpu.VMEM((rows, wa), BF16),
                        pltpu.VMEM((page, wa), F32), pltpu.VMEM((page, wa), F32),
                        pltpu.VMEM((rows, 1), F32), pltpu.VMEM((rows, 1), F32),
                        pltpu.VMEM((rows, dv), F32)])
    return pl.pallas_call(
        functools.partial(_attn_sample_kernel, pps=pps, heads=heads, tq=tq,
                          past_len=past_len, page=page),
        grid_spec=grid_spec,
        out_shape=jax.ShapeDtypeStruct((db, tq, wa), F32),
        compiler_params=_params("parallel", "arbitrary"),
        name="attn_sample",
    )(page_table.reshape(-1), *([pk] * pps), *([pv] * pps), q, k_new, v_new,
      slope_rows, qidx_rows, qmask, lam_vecs, g_subln.reshape(1, dv))


def _s5_kernel(u_ref, x0_ref, bd_ref, cd_ref, a_ref, d_ref, y_ref, xf_ref, xs_ref, st_ref,
               *, nbb, tbs, half_in, half_state):
    c = pl.program_id(1)

    @pl.when(c == 0)
    def _():
        st_ref[...] = x0_ref[...]

    sub, lanes = V7X_SUBLANES, V7X_LANES
    rows = tbs * nbb * sub
    width = u_ref.shape[-1]
    u = u_ref[...].reshape(rows, width)
    ub = u.astype(BF16)
    hs = half_state
    n_sl = hs // lanes
    for k in range(2):
        res = _dot(ub[:, k * half_in:(k + 1) * half_in], bd_ref[k])
        for s in range(2 * n_sl):
            xs_ref[2 * k * n_sl + s] = res[:, s * lanes:(s + 1) * lanes]

    for k in range(2):
        re0, im0 = 2 * k * hs, (2 * k + 1) * hs
        sl_re, sl_im = 2 * k * n_sl, (2 * k + 1) * n_sl
        a_re = a_ref[2 * k]
        a_im = a_ref[2 * k + 1]

        def group(r, _):
            rows0 = pl.multiple_of(r * sub, sub)
            x_re = st_ref[pl.ds(rows0, sub), re0:re0 + hs]
            x_im = st_ref[pl.ds(rows0, sub), im0:im0 + hs]

            def step(t, carry):
                x_re, x_im = carry
                start = (lax.shift_right_logical(t, 3) * (nbb * sub) + r * (sub * sub)
                         + lax.bitwise_and(t, sub - 1))
                rws = pl.ds(start, sub, stride=sub)
                b_re = jnp.concatenate([xs_ref[sl_re + s, rws, :] for s in range(n_sl)], axis=1)
                b_im = jnp.concatenate([xs_ref[sl_im + s, rws, :] for s in range(n_sl)], axis=1)
                n_re = a_re * x_re - a_im * x_im + b_re
                n_im = a_re * x_im + a_im * x_re + b_im
                for s in range(n_sl):
                    xs_ref[sl_re + s, rws, :] = n_re[:, s * lanes:(s + 1) * lanes]
                    xs_ref[sl_im + s, rws, :] = n_im[:, s * lanes:(s + 1) * lanes]
                return n_re, n_im

            x_re, x_im = lax.fori_loop(0, tbs * sub, step, (x_re, x_im))
            st_ref[pl.ds(rows0, sub), re0:re0 + hs] = x_re
            st_ref[pl.ds(rows0, sub), im0:im0 + hs] = x_im
            return 0

        lax.fori_loop(0, nbb // sub, group, 0)

    ys = []
    for k in range(2):
        xk = jnp.concatenate([xs_ref[2 * k * n_sl + s] for s in range(2 * n_sl)], axis=1)
        ys.append(_dot(xk.astype(BF16), cd_ref[k]))
    y = jnp.concatenate(ys, axis=1) + d_ref[...] * u
    y_ref[...] = y.reshape(y_ref.shape)

    @pl.when(c == pl.num_programs(1) - 1)
    def _():
        xf_ref[...] = st_ref[...]


def _s5(u4, x0, bd, cd, a_coef, d_vec):
    n_tb, nb, sub, width = u4.shape
    nbb = min(nb, S5_BATCH_BLOCK)
    tbs = min(n_tb, max(1, S5_CHUNK_ROWS // (nbb * sub)))
    n_state = x0.shape[1]
    tok = pl.BlockSpec((tbs, nbb, sub, width), lambda g, c: (c, g, 0, 0))
    st = pl.BlockSpec((nbb, n_state), lambda g, c: (g, 0))
    return pl.pallas_call(
        functools.partial(_s5_kernel, nbb=nbb, tbs=tbs, half_in=width // 2,
                          half_state=n_state // 4),
        grid=(nb // nbb, n_tb // tbs),
        in_specs=[tok, st, _full(bd.shape), _full(cd.shape), _full(a_coef.shape),
                  _full((1, width))],
        out_specs=[tok, st],
        out_shape=[jax.ShapeDtypeStruct(u4.shape, F32), jax.ShapeDtypeStruct(x0.shape, F32)],
        scratch_shapes=[pltpu.VMEM((n_state // V7X_LANES, tbs * nbb * sub, V7X_LANES), F32),
                        pltpu.VMEM((nbb, n_state), F32)],
        compiler_params=_params("parallel", "arbitrary"),
        name="s5_scan",
    )(u4, x0, bd, cd, a_coef, d_vec.reshape(1, width))


def _s5_weights(a_re, a_im, b_re, b_im, c_re, c_im, log_dt):
    g, p, ch = b_re.shape
    dt = jnp.exp(log_dt)[:, None]
    mag = jnp.exp(dt * a_re)
    abar_re, abar_im = mag * jnp.cos(dt * a_im), mag * jnp.sin(dt * a_im)
    den = a_re * a_re + a_im * a_im
    f_re = ((abar_re - 1.0) * a_re + abar_im * a_im) / den
    f_im = (abar_im * a_re - (abar_re - 1.0) * a_im) / den
    bbar_re = f_re[..., None] * b_re - f_im[..., None] * b_im
    bbar_im = f_re[..., None] * b_im + f_im[..., None] * b_re
    gh = g // 2
    eye = jnp.eye(gh, dtype=F32)

    def bd_half(w):
        return jnp.einsum('gpc,gh->gchp', w, eye).reshape(gh * ch, gh * p)

    def cd_half(w):
        return jnp.einsum('gcp,gh->gphc', w, eye).reshape(gh * p, gh * ch)

    bd = jnp.stack([jnp.concatenate([bd_half(bbar_re[k * gh:(k + 1) * gh]),
                                     bd_half(bbar_im[k * gh:(k + 1) * gh])], axis=1)
                    for k in range(2)]).astype(BF16)
    cd = jnp.stack([jnp.concatenate([cd_half(c_re[k * gh:(k + 1) * gh]),
                                     cd_half(-c_im[k * gh:(k + 1) * gh])], axis=0)
                    for k in range(2)]).astype(BF16)
    coef = [w[k * gh:(k + 1) * gh].reshape(1, gh * p) for k in range(2) for w in (abar_re, abar_im)]
    a_coef = jnp.broadcast_to(jnp.stack(coef), (4, V7X_SUBLANES, gh * p))
    return bd, cd, a_coef


def _state_pack(x_re, x_im):
    nb, g, p = x_re.shape
    hs = g * p // 2
    re = x_re.reshape(nb, 2, hs)
    im = x_im.reshape(nb, 2, hs)
    return jnp.stack([re[:, 0], im[:, 0], re[:, 1], im[:, 1]], axis=1).reshape(nb, 4 * hs)


def _state_unpack(x, g, p):
    nb = x.shape[0]
    hs = g * p // 2
    x = x.reshape(nb, 4, hs)
    re = jnp.stack([x[:, 0], x[:, 2]], axis=1).reshape(nb, g, p)
    im = jnp.stack([x[:, 1], x[:, 3]], axis=1).reshape(nb, g, p)
    return re, im


def _merge_kernel(x_ref, oa_ref, ys_ref, gmix_ref, wg_ref, bg_ref, wglu_ref, bglu_ref,
                  wpa_ref, wps_ref, wout_ref, gcross_ref, wcq_ref, x1_ref, qx_ref, *, qscale):
    x = x_ref[...]
    d = x.shape[-1]
    h = _rmsnorm(x, gmix_ref[...]).astype(BF16)
    gate = jax.nn.sigmoid(_dot(h, wg_ref[...]) + bg_ref[...])
    z = jax.nn.gelu(ys_ref[...].reshape(x.shape[0], -1))
    o_s = z * jax.nn.sigmoid(_dot(z.astype(BF16), wglu_ref[...]) + bglu_ref[...])
    pa = _dot(oa_ref[...].astype(BF16), wpa_ref[...])
    ps = _dot(o_s.astype(BF16), wps_ref[...])
    mix = gate[:, :d] * pa + gate[:, d:] * ps
    x1 = x + _dot(mix.astype(BF16), wout_ref[...])
    x1_ref[...] = x1
    hc = _rmsnorm(x1, gcross_ref[...]).astype(BF16)
    qx_ref[...] = (_dot(hc, wcq_ref[...]) * qscale).astype(qx_ref.dtype)


def _merge(x, oa, ys, ys_spec, grid, tok_spec, weights, qscale, inter_dtype):
    (gmix, wg, bg, wglu, bglu, wpa, wps, wout, gcross, wcq) = weights
    d = x.shape[-1]
    wx = wcq.shape[1]
    wspecs = [_full(w.shape) for w in weights]
    return pl.pallas_call(
        functools.partial(_merge_kernel, qscale=qscale),
        grid=grid,
        in_specs=[tok_spec(d), tok_spec(oa.shape[-1]), ys_spec] + wspecs,
        out_specs=[tok_spec(d), tok_spec(wx)],
        out_shape=[jax.ShapeDtypeStruct(x.shape, F32),
                   jax.ShapeDtypeStruct(x.shape[:-1] + (wx,), inter_dtype)],
        compiler_params=_params(*(("parallel",) * len(grid))),
        name="merge",
    )(x, oa, ys, *weights)


def _cross_kernel(q_ref, k_ref, v_ref, o_ref, *, heads):
    dx = q_ref.shape[-1] // heads
    n_mem = k_ref.shape[0] // heads
    for h in range(heads):
        sl = slice(h * dx, (h + 1) * dx)
        rows = pl.ds(h, n_mem, stride=heads)
        q = q_ref[:, sl].astype(BF16)
        s = _dot_nt(q, k_ref[rows, :].astype(BF16))
        m = jnp.max(s, axis=-1, keepdims=True)
        p = jnp.exp(s - m)
        l = jnp.sum(p, axis=-1, keepdims=True)
        o = _dot(p.astype(BF16), v_ref[rows, :].astype(BF16)) / l
        o_ref[:, sl] = o.astype(o_ref.dtype)


def _cross(qx, mem_k, mem_v, heads, tm):
    b, t, wx = qx.shape
    rows, dx = mem_k.shape[1:]
    tok = pl.BlockSpec((None, tm, wx), lambda bi, i: (bi, i, 0))
    mem = pl.BlockSpec((None, rows, dx), lambda bi, i: (bi, 0, 0))
    return pl.pallas_call(
        functools.partial(_cross_kernel, heads=heads),
        grid=(b, t // tm),
        in_specs=[tok, mem, mem],
        out_specs=tok,
        out_shape=jax.ShapeDtypeStruct(qx.shape, qx.dtype),
        compiler_params=_params("parallel", "parallel"),
        name="cross_attn",
    )(qx, mem_k, mem_v)


def _ffn_kernel(x_ref, ca_ref, wco_ref, gffn_ref, wfi_ref, wfo_ref, gfin_ref, y_ref, *, dff):
    x2 = x_ref[...] + _dot(ca_ref[...].astype(BF16), wco_ref[...])
    h = _rmsnorm(x2, gffn_ref[...]).astype(BF16)
    ff = _dot(h, wfi_ref[...])
    hid = jax.nn.silu(ff[:, :dff]) * ff[:, dff:]
    x3 = x2 + _dot(hid.astype(BF16), wfo_ref[...])
    y_ref[...] = _rmsnorm(x3, gfin_ref[...])


def _ffn(x1, ca, weights, tm):
    (wco, gffn, wfi, wfo, gfin) = weights
    n, d = x1.shape
    tm = min(tm, n)
    tok = lambda w: pl.BlockSpec((tm, w), lambda i: (i, 0))
    return pl.pallas_call(
        functools.partial(_ffn_kernel, dff=wfo.shape[0]),
        grid=(n // tm,),
        in_specs=[tok(d), tok(ca.shape[-1])] + [_full(w.shape) for w in weights],
        out_specs=tok(d),
        out_shape=jax.ShapeDtypeStruct((n, d), F32),
        compiler_params=_params("parallel"),
        name="ffn",
    )(x1, ca, *weights)


def kernel(x_prompt, x_sample, mem_prompt, cache_k, cache_v, state_ssm_re, state_ssm_im, cache_mem_k, cache_mem_v, page_table, g_mix, w_in, b_gate, lambda_q1, lambda_k1, lambda_q2, lambda_k2, g_subln, ssm_a_re, ssm_a_im, ssm_b_re, ssm_b_im, ssm_c_re, ssm_c_im, ssm_d, ssm_log_dt, w_glu, b_glu, w_proj_a, w_proj_s, w_out, g_cross, g_mem, w_cq, w_mem_kv, w_co, g_ffn, w_ff_in, w_ff_out, g_final):
    assert g_mix.shape[0] == 1, "single-layer trunk"
    b, t, d = x_prompt.shape
    db, dt_len, _ = x_sample.shape
    heads = cache_k.shape[3]
    dk = cache_k.shape[5]
    dv = cache_v.shape[4]
    wa = heads * dv
    g, p, _ = ssm_b_re.shape[1:]
    ws = ssm_d.shape[1]
    n_mem, heads_x, dx = cache_mem_k.shape[2:]
    wx = heads_x * dx
    row2 = lambda v: v.reshape(1, -1)

    w_in0 = w_in[0]
    qs = dk ** -0.5
    w_q, w_k, w_v, w_u = (w_in0[:, :wa] * qs, w_in0[:, wa:2 * wa],
                          w_in0[:, 2 * wa:3 * wa], w_in0[:, 3 * wa:3 * wa + ws])
    w_gate = w_in0[:, 3 * wa + ws:].astype(BF16)
    w_kvu = jnp.concatenate([w_k, w_v, w_u], axis=1).astype(BF16)
    w_qvt = (w_q * LOG2E).T.astype(BF16)
    w_qkvu = jnp.concatenate([w_q, w_k, w_v, w_u], axis=1).astype(BF16)
    lam_vecs = jnp.concatenate([lambda_q1, lambda_k1, lambda_q2, lambda_k2], axis=0)
    slopes_np = [2.0 ** (-8.0 * (h + 1) / heads) for h in range(heads)]
    bd, cd, a_coef = _s5_weights(ssm_a_re[0], ssm_a_im[0], ssm_b_re[0], ssm_b_im[0],
                                 ssm_c_re[0], ssm_c_im[0], ssm_log_dt[0])
    merge_w = (row2(g_mix[0]), w_gate, row2(b_gate[0]), w_glu[0].astype(BF16), row2(b_glu[0]),
               w_proj_a[0].astype(BF16), w_proj_s[0].astype(BF16), w_out[0].astype(BF16),
               row2(g_cross[0]), w_cq[0].astype(BF16))
    ffn_w = (w_co[0].astype(BF16), row2(g_ffn[0]), w_ff_in[0].astype(BF16),
             w_ff_out[0].astype(BF16), row2(g_final))
    xscale = dx ** -0.5

    mkv = _norm_matmul(mem_prompt.reshape(b * n_mem, d), g_mem[0], w_mem_kv[0].astype(BF16), 512)
    mk = mkv[:, :wx].reshape(b, n_mem, wx)
    mv = mkv[:, wx:].reshape(b, n_mem, wx)
    k32, v32, kbf, u_p, qt, vt = _in_proj_prompt(x_prompt, g_mix[0], w_kvu, w_qvt, wa, ws)
    oa_p = _attn_prompt(qt, kbf, vt, lam_vecs, g_subln[0], slopes_np, heads)
    zeros = jnp.zeros((b, g, p), F32)
    ys_p, xf_p = _s5(u_p, _state_pack(zeros, zeros), bd, cd, a_coef, ssm_d[0])
    tm = MERGE_TOKENS
    sub = V7X_SUBLANES
    tok3 = lambda w: pl.BlockSpec((None, tm, w), lambda bi, i: (bi, i, 0))
    ys_spec = pl.BlockSpec((tm // sub, None, sub, ws), lambda bi, i: (i, bi, 0, 0))
    x1_p, qx_p = _merge(x_prompt, oa_p, ys_p, ys_spec, (b, t // tm), tok3, merge_w, xscale, BF16)
    ca_p = _cross(qx_p, mk.reshape(b, n_mem * heads_x, dx), mv.reshape(b, n_mem * heads_x, dx),
                  heads_x, CROSS_TOKENS)
    y_prompt = _ffn(x1_p.reshape(b * t, d), ca_p.reshape(b * t, wx), ffn_w, FFN_TOKENS).reshape(b, t, d)
    srp, sip = _state_unpack(xf_p, g, p)

    n_s = db * dt_len
    q_s, k_s, v_s, u_s = _in_proj_sample(x_sample.reshape(n_s, d), g_mix[0], w_qkvu, wa, ws)
    oa_s = _attn_sample(q_s.reshape(db, dt_len, wa), k_s.reshape(db, dt_len, wa),
                        v_s.reshape(db, dt_len, wa), cache_k[0], cache_v[0], page_table,
                        lam_vecs, g_subln[0], slopes_np, heads)
    assert dt_len == sub, "sample rows (batch, step) are the S5 row order only for 8 new tokens"
    ys_s, xf_s = _s5(u_s.reshape(1, db, sub, ws),
                     _state_pack(state_ssm_re[0], state_ssm_im[0]), bd, cd, a_coef, ssm_d[0])
    ys_s = ys_s.reshape(n_s, ws)
    tok2 = lambda w: pl.BlockSpec((n_s, w), lambda i: (0, 0))
    x1_s, qx_s = _merge(x_sample.reshape(n_s, d), oa_s.reshape(n_s, wa), ys_s, tok2(ws), (1,),
                        tok2, merge_w, xscale, F32)
    ca_s = _cross(qx_s.reshape(db, dt_len, wx), cache_mem_k[0].reshape(db, n_mem * heads_x, dx),
                  cache_mem_v[0].reshape(db, n_mem * heads_x, dx), heads_x, dt_len)
    y_sample = _ffn(x1_s, ca_s.reshape(n_s, wx), ffn_w, FFN_TOKENS).reshape(db, dt_len, d)
    srs, sis = _state_unpack(xf_s, g, p)

    return (y_prompt, y_sample,
            k32.reshape(1, b, t, heads, 2, dk), v32.reshape(1, b, t, heads, dv),
            k_s.reshape(1, db, dt_len, heads, 2, dk), v_s.reshape(1, db, dt_len, heads, dv),
            srp[None], sip[None], srs[None], sis[None],
            mk.reshape(1, b, n_mem, heads_x, dx), mv.reshape(1, b, n_mem, heads_x, dx))
```

```python
import functools
import math

import numpy as np
import jax
import jax.numpy as jnp
from jax import lax
from jax.experimental import pallas as pl
from jax.experimental.pallas import tpu as pltpu

F32 = jnp.float32
BF16 = jnp.bfloat16
EPS = 1e-5
LOG2E = math.log2(math.e)
LAM_INIT = 0.8 - 0.6 * math.exp(-0.3 * 0)

V7X_LANES = 128
V7X_SUBLANES = 8
V7X_VMEM_LIMIT_BYTES = 48 * 1024 * 1024

IN_PROJ_TOKENS = 512
ATTN_Q_BLOCK = 512
ATTN_UNROLL = 4
SAMPLE_PAGES_PER_STEP = 16
S5_CHUNK_ROWS = 512
S5_BATCH_BLOCK = 64
MERGE_TOKENS = 512
FFN_TOKENS = 256
CROSS_TOKENS = 512


def _params(*sem):
    return pltpu.CompilerParams(dimension_semantics=sem,
                                vmem_limit_bytes=V7X_VMEM_LIMIT_BYTES)


def _rmsnorm(x, g):
    r = lax.rsqrt(jnp.mean(x * x, axis=-1, keepdims=True) + EPS)
    return (x * r) * g


def _dot(a, b):
    return jnp.dot(a, b, preferred_element_type=F32)


def _dot_nt(a, b):
    return lax.dot_general(a, b, (((1,), (1,)), ((), ())), preferred_element_type=F32)


def _full(shape):
    zeros = (0,) * len(shape)
    return pl.BlockSpec(shape, lambda *_: zeros)


def _norm_matmul_kernel(x_ref, g_ref, w_ref, o_ref):
    h = _rmsnorm(x_ref[...], g_ref[...]).astype(BF16)
    o_ref[...] = _dot(h, w_ref[...])


def _norm_matmul(x, g, w_bf16, tm):
    n, d = x.shape
    n_out = w_bf16.shape[1]
    return pl.pallas_call(
        _norm_matmul_kernel,
        grid=(n // tm,),
        in_specs=[pl.BlockSpec((tm, d), lambda i: (i, 0)), _full((1, d)), _full((d, n_out))],
        out_specs=pl.BlockSpec((tm, n_out), lambda i: (i, 0)),
        out_shape=jax.ShapeDtypeStruct((n, n_out), F32),
        compiler_params=_params("parallel"),
        name="norm_matmul",
    )(x, g.reshape(1, d), w_bf16)


def _in_proj_prompt_kernel(x_ref, g_ref, wkvu_ref, wqvt_ref,
                           k32_ref, v32_ref, kbf_ref, u_ref, qt_ref, vt_ref, *, wa, heads):
    h = _rmsnorm(x_ref[...], g_ref[...]).astype(BF16)
    r = _dot(h, wkvu_ref[...])
    k = r[:, :wa]
    k32_ref[...] = k
    kbf_ref[...] = k.astype(BF16)
    v = r[:, wa:2 * wa]
    dv = wa // heads
    for hd in range(heads):
        v32_ref[pl.ds(hd, v.shape[0], stride=heads), :] = v[:, hd * dv:(hd + 1) * dv]
    vt_ref[...] = v.T.astype(BF16)
    u_ref[...] = r[:, 2 * wa:].reshape(u_ref.shape)
    qt_ref[...] = _dot_nt(wqvt_ref[...], h).astype(BF16)


def _in_proj_prompt(x, g, wkvu, wqvt, wa, ws, heads):
    b, t, d = x.shape
    tm = IN_PROJ_TOKENS
    sub = V7X_SUBLANES
    row = lambda bi, i: (bi, i, 0)
    col = lambda bi, i: (bi, 0, i)
    return pl.pallas_call(
        functools.partial(_in_proj_prompt_kernel, wa=wa, heads=heads),
        grid=(b, t // tm),
        in_specs=[pl.BlockSpec((None, tm, d), row), _full((1, d)),
                  _full(wkvu.shape), _full(wqvt.shape)],
        out_specs=[pl.BlockSpec((None, tm, wa), row),
                   pl.BlockSpec((None, tm * heads, wa // heads), row),
                   pl.BlockSpec((None, tm, wa), row),
                   pl.BlockSpec((tm // sub, None, sub, ws), lambda bi, i: (i, bi, 0, 0)),
                   pl.BlockSpec((None, wa, tm), col), pl.BlockSpec((None, wa, tm), col)],
        out_shape=[jax.ShapeDtypeStruct((b, t, wa), F32),
                   jax.ShapeDtypeStruct((b, t * heads, wa // heads), F32),
                   jax.ShapeDtypeStruct((b, t, wa), BF16),
                   jax.ShapeDtypeStruct((t // sub, b, sub, ws), F32),
                   jax.ShapeDtypeStruct((b, wa, t), BF16), jax.ShapeDtypeStruct((b, wa, t), BF16)],
        compiler_params=_params("parallel", "parallel"),
        name="in_proj_prompt",
    )(x, g.reshape(1, d), wkvu, wqvt)


def _in_proj_sample_kernel(x_ref, g_ref, w_ref, q_ref, k_ref, v_ref, u_ref, *, wa):
    h = _rmsnorm(x_ref[...], g_ref[...]).astype(BF16)
    r = _dot(h, w_ref[...])
    q_ref[...] = r[:, :wa]
    k_ref[...] = r[:, wa:2 * wa]
    v_ref[...] = r[:, 2 * wa:3 * wa]
    u_ref[...] = r[:, 3 * wa:]


def _in_proj_sample(x, g, w, wa, ws):
    n, d = x.shape
    shapes = [jax.ShapeDtypeStruct((n, wa), F32)] * 3 + [jax.ShapeDtypeStruct((n, ws), F32)]
    return pl.pallas_call(
        functools.partial(_in_proj_sample_kernel, wa=wa),
        grid=(1,),
        in_specs=[_full((n, d)), _full((1, d)), _full(w.shape)],
        out_specs=[_full(s.shape) for s in shapes],
        out_shape=shapes,
        compiler_params=_params("arbitrary"),
        name="in_proj_sample",
    )(x, g.reshape(1, d), w)


def _lambda_full(lam_ref):
    lv = lam_ref[...]
    e1 = jnp.exp(jnp.sum(lv[0:1] * lv[1:2], axis=-1, keepdims=True))
    e2 = jnp.exp(jnp.sum(lv[2:3] * lv[3:4], axis=-1, keepdims=True))
    return e1 - e2 + LAM_INIT


def _attn_prompt_kernel(slopes_ref, qt_ref, k_ref, vt_ref, lam_ref, gsub_ref, o_ref,
                        acc_ref, m_ref, sa_ref, sb_ref, *, blk, dk, ones_rows, unroll):
    i = pl.program_id(2)
    slope = slopes_ref[pl.program_id(1)]
    dv = vt_ref.shape[0]
    qt = qt_ref[...]
    zero = jnp.zeros((dk, blk), BF16)
    qm = (jnp.concatenate([qt[:dk], zero], axis=0), jnp.concatenate([zero, qt[dk:]], axis=0))
    v_ones = jnp.ones((ones_rows, blk), BF16)
    krel = lax.broadcasted_iota(jnp.int32, (blk, blk), 0)
    qrel = lax.broadcasted_iota(jnp.int32, (blk, blk), 1)
    kbias = slope * krel.astype(F32)

    def scores(j):
        k = k_ref[pl.ds(pl.multiple_of(j * blk, blk), blk), :]
        return tuple(_dot(k, q) + kbias for q in qm)

    def fold(s_ref, j, masked):
        off = slope * ((j - i) * blk).astype(F32)
        vt = vt_ref[:, pl.ds(pl.multiple_of(j * blk, blk), blk)]
        vta = jnp.concatenate([vt, v_ones], axis=0)
        for mp in range(2):
            s = s_ref[mp]
            if masked:
                s = jnp.where(krel <= qrel, s, -jnp.inf)
            m = m_ref[mp:mp + 1]
            m_new = jnp.maximum(m, jnp.max(s, axis=0, keepdims=True) + off)
            p = jnp.exp2(s - (m_new - off)).astype(BF16)
            acc_ref[mp] = jnp.exp2(m - m_new) * acc_ref[mp] + _dot(vta, p)
            m_ref[mp:mp + 1] = m_new

    def stage(j, s_cur_ref, s_next_ref):
        for mp, s in enumerate(scores(j + 1)):
            s_next_ref[mp] = s
        fold(s_cur_ref, j, False)

    s_bufs = (sa_ref, sb_ref)

    def unrolled(n, _):
        for u in range(unroll):
            stage(unroll * n + u, s_bufs[u % 2], s_bufs[(u + 1) % 2])
        return 0

    acc_ref[...] = jnp.zeros_like(acc_ref)
    m_ref[...] = jnp.full_like(m_ref, -jnp.inf)
    for mp, s in enumerate(scores(0)):
        sa_ref[mp] = s
    n_full = i // unroll
    lax.fori_loop(0, n_full, unrolled, 0)
    rest = i - n_full * unroll
    for u in range(unroll - 1):
        @pl.when(rest > u)
        def _():
            stage(n_full * unroll + u, s_bufs[u % 2], s_bufs[(u + 1) % 2])

    for parity in range(2):
        @pl.when(rest % 2 == parity)
        def _():
            fold(s_bufs[parity], i, True)

    lam = _lambda_full(lam_ref)
    a0 = acc_ref[0]
    a1 = acc_ref[1]
    o = a0[:dv] / a0[dv:dv + 1] - lam * (a1[:dv] / a1[dv:dv + 1])
    r = lax.rsqrt(jnp.mean(o * o, axis=0, keepdims=True) + EPS)
    o = (o * r) * gsub_ref[...] * (1.0 - LAM_INIT)
    o_ref[...] = o.T.astype(o_ref.dtype)


def _attn_prompt(qt, kbf, vt, lam_vecs, g_subln, slopes_np, heads):
    b, wa, t = qt.shape
    dv = wa // heads
    dk = dv // 2
    blk = ATTN_Q_BLOCK
    ones_rows = 2 * V7X_SUBLANES
    slopes = jnp.asarray(slopes_np, F32) * LOG2E
    return pl.pallas_call(
        functools.partial(_attn_prompt_kernel, blk=blk, dk=dk, ones_rows=ones_rows,
                          unroll=ATTN_UNROLL),
        grid=(b, heads, t // blk),
        in_specs=[pl.BlockSpec(memory_space=pltpu.SMEM),
                  pl.BlockSpec((None, dv, blk), lambda bi, h, i: (bi, h, i)),
                  pl.BlockSpec((None, t, dv), lambda bi, h, i: (bi, 0, h)),
                  pl.BlockSpec((None, dv, t), lambda bi, h, i: (bi, h, 0)),
                  _full(lam_vecs.shape), _full((dv, 1))],
        out_specs=pl.BlockSpec((None, blk, dv), lambda bi, h, i: (bi, i, h)),
        out_shape=jax.ShapeDtypeStruct((b, t, wa), BF16),
        scratch_shapes=[pltpu.VMEM((2, dv + ones_rows, blk), F32), pltpu.VMEM((2, blk), F32),
                        pltpu.VMEM((2, blk, blk), F32), pltpu.VMEM((2, blk, blk), F32)],
        compiler_params=_params("parallel", "parallel", "arbitrary"),
        name="attn_prompt",
    )(slopes, qt, kbf, vt, lam_vecs, g_subln.reshape(dv, 1))


def _attn_sample_kernel(pt_ref, *refs, pps, heads, tq, past_len, page):
    k_pages = refs[:pps]
    v_pages = refs[pps:2 * pps]
    (q_ref, kn_ref, vn_ref, slope_ref, qidx_ref, qmask_ref, lam_ref, gsub_ref, o_ref,
     qbd_ref, kpad_ref, vpad_ref, m_ref, l_ref, acc_ref) = refs[2 * pps:]
    c = pl.program_id(1)
    hr = 2 * tq
    rows = heads * hr
    dv = q_ref.shape[-1] // heads

    slope = slope_ref[...]
    qidx = qidx_ref[...]
    lane = lax.broadcasted_iota(jnp.int32, (rows, page), 1)

    def fold(s, head_vals):
        m_old = m_ref[...]
        m_new = jnp.maximum(m_old, jnp.max(s, axis=-1, keepdims=True))
        p = jnp.exp(s - m_new)
        alpha = jnp.exp(m_old - m_new)
        l_ref[...] = alpha * l_ref[...] + jnp.sum(p, axis=-1, keepdims=True)
        pb = p.astype(BF16)
        for h in range(heads):
            r = slice(h * hr, (h + 1) * hr)
            pv = _dot(pb[r, :page], head_vals(0, h))
            for j in range(1, s.shape[1] // page):
                pv += _dot(pb[r, j * page:(j + 1) * page], head_vals(j, h))
            acc_ref[r, :] = alpha[r] * acc_ref[r, :] + pv
        m_ref[...] = m_new

    @pl.when(c == 0)
    def _():
        q = q_ref[...]
        qt = jnp.concatenate([q] * (rows // tq), axis=0)
        qbd_ref[...] = (qt * qmask_ref[...]).astype(BF16)
        m_ref[...] = jnp.full_like(m_ref, -jnp.inf)
        l_ref[...] = jnp.zeros_like(l_ref)
        acc_ref[...] = jnp.zeros_like(acc_ref)
        kpad_ref[...] = jnp.zeros_like(kpad_ref)
        vpad_ref[...] = jnp.zeros_like(vpad_ref)
        kpad_ref[0:tq, :] = kn_ref[...]
        vpad_ref[0:tq, :] = vn_ref[...]
        s = _dot_nt(qbd_ref[...], kpad_ref[...].astype(BF16))
        s = s - slope * (qidx - lane).astype(F32)
        s = jnp.where(lane <= qidx, s, -jnp.inf)
        fold(s, lambda j, h: vpad_ref[:, h * dv:(h + 1) * dv].astype(BF16))

    qbd = qbd_ref[...]
    qpos = past_len + qidx
    parts = []
    for j in range(pps):
        s = _dot(qbd, k_pages[j][...].astype(BF16))
        kpos = (c * pps + j) * page + lane
        parts.append(s - slope * (qpos - kpos).astype(F32))
    fold(jnp.concatenate(parts, axis=1),
         lambda j, h: v_pages[j][pl.ds(h, page, stride=heads), :].astype(BF16))

    @pl.when(c == pl.num_programs(1) - 1)
    def _():
        lam = _lambda_full(lam_ref)
        o = acc_ref[...] / l_ref[...]
        for h in range(heads):
            d = o[h * hr:h * hr + tq] - lam * o[h * hr + tq:(h + 1) * hr]
            d = _rmsnorm(d, gsub_ref[...]) * (1.0 - LAM_INIT)
            o_ref[:, h * dv:(h + 1) * dv] = d.astype(o_ref.dtype)


def _attn_sample(q, k_new, v_new, pool_k, pool_v, page_table, lam_vecs, g_subln, slopes_np, heads):
    db, tq, wa = q.shape
    n_phys, page = pool_k.shape[0], pool_k.shape[1]
    n_pages = page_table.shape[1]
    past_len = n_pages * page
    pps = math.gcd(SAMPLE_PAGES_PER_STEP, n_pages)
    dv = wa // heads
    dk = dv // 2
    rows = heads * 2 * tq
    pk = jnp.transpose(pool_k, (0, 2, 3, 4, 1)).reshape(n_phys, wa, page)
    pv = pool_v.reshape(n_phys, page * heads, dv)
    r_idx = np.arange(rows)
    slope_rows = jnp.asarray(np.repeat(np.asarray(slopes_np)[r_idx // (2 * tq)][:, None], page, 1), F32)
    qidx_rows = jnp.asarray(np.repeat((r_idx % tq)[:, None], page, 1), jnp.int32)
    qmask = jnp.asarray((r_idx // tq)[:, None] == (np.arange(wa) // dk)[None, :], F32)

    def page_spec(j, shape):
        return pl.BlockSpec((None,) + shape,
                            lambda b, c, pt: (pt[b * n_pages + c * pps + j], 0, 0))

    per_b = lambda b, c, pt: (b, 0, 0)
    const2 = lambda b, c, pt: (0, 0)
    grid_spec = pltpu.PrefetchScalarGridSpec(
        num_scalar_prefetch=1,
        grid=(db, n_pages // pps),
        in_specs=([page_spec(j, (wa, page)) for j in range(pps)]
                  + [page_spec(j, (page * heads, dv)) for j in range(pps)]
                  + [pl.BlockSpec((None, tq, wa), per_b)] * 3
                  + [pl.BlockSpec((rows, page), const2), pl.BlockSpec((rows, page), const2),
                     pl.BlockSpec((rows, wa), const2),
                     pl.BlockSpec(lam_vecs.shape, const2),
                     pl.BlockSpec((1, dv), const2)]),
        out_specs=pl.BlockSpec((None, tq, wa), per_b),
        scratch_shapes=[pltpu.VMEM((rows, wa), BF16),
                        pltpu.VMEM((page, wa), F32), pltpu.VMEM((page, wa), F32),
                        pltpu.VMEM((rows, 1), F32), pltpu.VMEM((rows, 1), F32),
                        pltpu.VMEM((rows, dv), F32)])
    return pl.pallas_call(
        functools.partial(_attn_sample_kernel, pps=pps, heads=heads, tq=tq,
                          past_len=past_len, page=page),
        grid_spec=grid_spec,
        out_shape=jax.ShapeDtypeStruct((db, tq, wa), F32),
        compiler_params=_params("parallel", "arbitrary"),
        name="attn_sample",
    )(page_table.reshape(-1), *([pk] * pps), *([pv] * pps), q, k_new, v_new,
      slope_rows, qidx_rows, qmask, lam_vecs, g_subln.reshape(1, dv))


def _s5_kernel(u_ref, x0_ref, bd_ref, cd_ref, a_ref, d_ref, y_ref, xf_ref,
               us_ref, xs_ref, ys_ref, st_ref, *, nbb, tbs, half_in, half_state):
    c = pl.program_id(1)

    @pl.when(c == 0)
    def _():
        st_ref[...] = x0_ref[...]

    sub, lanes = V7X_SUBLANES, V7X_LANES
    width = u_ref.shape[-1]
    n_us = width // lanes
    hs = half_state

    def batch_rows(tb, bb):
        return pl.ds(tb * sub * nbb + bb, sub, stride=nbb)

    for tb in range(tbs):
        for bb in range(nbb):
            ut = u_ref[tb, bb]
            for s in range(n_us):
                us_ref[s, batch_rows(tb, bb), :] = ut[:, s * lanes:(s + 1) * lanes]
    ub = jnp.concatenate([us_ref[s] for s in range(n_us)], axis=1).astype(BF16)
    for k in range(2):
        xs_ref[:, 2 * k * hs:2 * (k + 1) * hs] = _dot(
            ub[:, k * half_in:(k + 1) * half_in], bd_ref[k])

    for k in range(2):
        re0, im0 = 2 * k * hs, (2 * k + 1) * hs
        a_re = a_ref[2 * k]
        a_im = a_ref[2 * k + 1]

        def group(r, _):
            rows0 = pl.multiple_of(r * sub, sub)
            x_re = st_ref[pl.ds(rows0, sub), re0:re0 + hs]
            x_im = st_ref[pl.ds(rows0, sub), im0:im0 + hs]

            def step(t, carry):
                x_re, x_im = carry
                row = pl.multiple_of(t * nbb + rows0, sub)
                n_re = a_re * x_re - a_im * x_im + xs_ref[pl.ds(row, sub), re0:re0 + hs]
                n_im = a_re * x_im + a_im * x_re + xs_ref[pl.ds(row, sub), im0:im0 + hs]
                xs_ref[pl.ds(row, sub), re0:re0 + hs] = n_re
                xs_ref[pl.ds(row, sub), im0:im0 + hs] = n_im
                return n_re, n_im

            x_re, x_im = lax.fori_loop(0, tbs * sub, step, (x_re, x_im))
            st_ref[pl.ds(rows0, sub), re0:re0 + hs] = x_re
            st_ref[pl.ds(rows0, sub), im0:im0 + hs] = x_im
            return 0

        lax.fori_loop(0, nbb // sub, group, 0)

    for k in range(2):
        yk = _dot(xs_ref[:, 2 * k * hs:2 * (k + 1) * hs].astype(BF16), cd_ref[k])
        for s in range(yk.shape[1] // lanes):
            ys_ref[k * (yk.shape[1] // lanes) + s] = yk[:, s * lanes:(s + 1) * lanes]
    d = d_ref[...]
    for tb in range(tbs):
        for bb in range(nbb):
            yt = jnp.concatenate([ys_ref[s, batch_rows(tb, bb), :] for s in range(n_us)], axis=1)
            y_ref[tb, bb] = yt + d * u_ref[tb, bb]

    @pl.when(c == pl.num_programs(1) - 1)
    def _():
        xf_ref[...] = st_ref[...]


def _s5(u4, x0, bd, cd, a_coef, d_vec):
    n_tb, nb, sub, width = u4.shape
    nbb = min(nb, S5_BATCH_BLOCK)
    tbs = min(n_tb, max(1, S5_CHUNK_ROWS // (nbb * sub)))
    rows = tbs * nbb * sub
    n_state = x0.shape[1]
    tok = pl.BlockSpec((tbs, nbb, sub, width), lambda g, c: (c, g, 0, 0))
    st = pl.BlockSpec((nbb, n_state), lambda g, c: (g, 0))
    slabs = pltpu.VMEM((width // V7X_LANES, rows, V7X_LANES), F32)
    return pl.pallas_call(
        functools.partial(_s5_kernel, nbb=nbb, tbs=tbs, half_in=width // 2,
                          half_state=n_state // 4),
        grid=(nb // nbb, n_tb // tbs),
        in_specs=[tok, st, _full(bd.shape), _full(cd.shape), _full(a_coef.shape),
                  _full((1, width))],
        out_specs=[tok, st],
        out_shape=[jax.ShapeDtypeStruct(u4.shape, F32), jax.ShapeDtypeStruct(x0.shape, F32)],
        scratch_shapes=[slabs, pltpu.VMEM((rows, n_state), F32), slabs,
                        pltpu.VMEM((nbb, n_state), F32)],
        compiler_params=_params("parallel", "arbitrary"),
        name="s5_scan",
    )(u4, x0, bd, cd, a_coef, d_vec.reshape(1, width))


def _s5_weights(a_re, a_im, b_re, b_im, c_re, c_im, log_dt):
    g, p, ch = b_re.shape
    dt = jnp.exp(log_dt)[:, None]
    mag = jnp.exp(dt * a_re)
    abar_re, abar_im = mag * jnp.cos(dt * a_im), mag * jnp.sin(dt * a_im)
    den = a_re * a_re + a_im * a_im
    f_re = ((abar_re - 1.0) * a_re + abar_im * a_im) / den
    f_im = (abar_im * a_re - (abar_re - 1.0) * a_im) / den
    bbar_re = f_re[..., None] * b_re - f_im[..., None] * b_im
    bbar_im = f_re[..., None] * b_im + f_im[..., None] * b_re
    gh = g // 2
    eye = jnp.eye(gh, dtype=F32)

    def bd_half(w):
        return jnp.einsum('gpc,gh->gchp', w, eye).reshape(gh * ch, gh * p)

    def cd_half(w):
        return jnp.einsum('gcp,gh->gphc', w, eye).reshape(gh * p, gh * ch)

    bd = jnp.stack([jnp.concatenate([bd_half(bbar_re[k * gh:(k + 1) * gh]),
                                     bd_half(bbar_im[k * gh:(k + 1) * gh])], axis=1)
                    for k in range(2)]).astype(BF16)
    cd = jnp.stack([jnp.concatenate([cd_half(c_re[k * gh:(k + 1) * gh]),
                                     cd_half(-c_im[k * gh:(k + 1) * gh])], axis=0)
                    for k in range(2)]).astype(BF16)
    coef = [w[k * gh:(k + 1) * gh].reshape(1, gh * p) for k in range(2) for w in (abar_re, abar_im)]
    a_coef = jnp.broadcast_to(jnp.stack(coef), (4, V7X_SUBLANES, gh * p))
    return bd, cd, a_coef


def _state_pack(x_re, x_im):
    nb, g, p = x_re.shape
    hs = g * p // 2
    re = x_re.reshape(nb, 2, hs)
    im = x_im.reshape(nb, 2, hs)
    return jnp.stack([re[:, 0], im[:, 0], re[:, 1], im[:, 1]], axis=1).reshape(nb, 4 * hs)


def _state_unpack(x, g, p):
    nb = x.shape[0]
    hs = g * p // 2
    x = x.reshape(nb, 4, hs)
    re = jnp.stack([x[:, 0], x[:, 2]], axis=1).reshape(nb, g, p)
    im = jnp.stack([x[:, 1], x[:, 3]], axis=1).reshape(nb, g, p)
    return re, im


def _merge_kernel(x_ref, oa_ref, ys_ref, gmix_ref, wg_ref, bg_ref, wglu_ref, bglu_ref,
                  wpa_ref, wps_ref, wout_ref, gcross_ref, wcq_ref, x1_ref, qx_ref, *, qscale):
    x = x_ref[...]
    d = x.shape[-1]
    h = _rmsnorm(x, gmix_ref[...]).astype(BF16)
    gate = jax.nn.sigmoid(_dot(h, wg_ref[...]) + bg_ref[...])
    z = jax.nn.gelu(ys_ref[...].reshape(x.shape[0], -1))
    o_s = z * jax.nn.sigmoid(_dot(z.astype(BF16), wglu_ref[...]) + bglu_ref[...])
    pa = _dot(oa_ref[...].astype(BF16), wpa_ref[...])
    ps = _dot(o_s.astype(BF16), wps_ref[...])
    mix = gate[:, :d] * pa + gate[:, d:] * ps
    x1 = x + _dot(mix.astype(BF16), wout_ref[...])
    x1_ref[...] = x1
    hc = _rmsnorm(x1, gcross_ref[...]).astype(BF16)
    qx_ref[...] = (_dot(hc, wcq_ref[...]) * qscale).astype(qx_ref.dtype)


def _merge(x, oa, ys, ys_spec, grid, tok_spec, weights, qscale, inter_dtype):
    (gmix, wg, bg, wglu, bglu, wpa, wps, wout, gcross, wcq) = weights
    d = x.shape[-1]
    wx = wcq.shape[1]
    wspecs = [_full(w.shape) for w in weights]
    return pl.pallas_call(
        functools.partial(_merge_kernel, qscale=qscale),
        grid=grid,
        in_specs=[tok_spec(d), tok_spec(oa.shape[-1]), ys_spec] + wspecs,
        out_specs=[tok_spec(d), tok_spec(wx)],
        out_shape=[jax.ShapeDtypeStruct(x.shape, F32),
                   jax.ShapeDtypeStruct(x.shape[:-1] + (wx,), inter_dtype)],
        compiler_params=_params(*(("parallel",) * len(grid))),
        name="merge",
    )(x, oa, ys, *weights)


def _cross_kernel(q_ref, k_ref, v_ref, o_ref, *, heads):
    dx = q_ref.shape[-1] // heads
    n_mem = k_ref.shape[0] // heads
    for h in range(heads):
        sl = slice(h * dx, (h + 1) * dx)
        rows = pl.ds(h, n_mem, stride=heads)
        q = q_ref[:, sl].astype(BF16)
        s = _dot_nt(q, k_ref[rows, :].astype(BF16))
        m = jnp.max(s, axis=-1, keepdims=True)
        p = jnp.exp(s - m)
        l = jnp.sum(p, axis=-1, keepdims=True)
        o = _dot(p.astype(BF16), v_ref[rows, :].astype(BF16)) / l
        o_ref[:, sl] = o.astype(o_ref.dtype)


def _cross(qx, mem_k, mem_v, heads, tm):
    b, t, wx = qx.shape
    rows, dx = mem_k.shape[1:]
    tok = pl.BlockSpec((None, tm, wx), lambda bi, i: (bi, i, 0))
    mem = pl.BlockSpec((None, rows, dx), lambda bi, i: (bi, 0, 0))
    return pl.pallas_call(
        functools.partial(_cross_kernel, heads=heads),
        grid=(b, t // tm),
        in_specs=[tok, mem, mem],
        out_specs=tok,
        out_shape=jax.ShapeDtypeStruct(qx.shape, qx.dtype),
        compiler_params=_params("parallel", "parallel"),
        name="cross_attn",
    )(qx, mem_k, mem_v)


def _ffn_kernel(x_ref, ca_ref, wco_ref, gffn_ref, wfi_ref, wfo_ref, gfin_ref, y_ref, *, dff):
    x2 = x_ref[...] + _dot(ca_ref[...].astype(BF16), wco_ref[...])
    h = _rmsnorm(x2, gffn_ref[...]).astype(BF16)
    ff = _dot(h, wfi_ref[...])
    hid = jax.nn.silu(ff[:, :dff]) * ff[:, dff:]
    x3 = x2 + _dot(hid.astype(BF16), wfo_ref[...])
    y_ref[...] = _rmsnorm(x3, gfin_ref[...])


def _ffn(x1, ca, weights, tm):
    (wco, gffn, wfi, wfo, gfin) = weights
    n, d = x1.shape
    tm = min(tm, n)
    tok = lambda w: pl.BlockSpec((tm, w), lambda i: (i, 0))
    return pl.pallas_call(
        functools.partial(_ffn_kernel, dff=wfo.shape[0]),
        grid=(n // tm,),
        in_specs=[tok(d), tok(ca.shape[-1])] + [_full(w.shape) for w in weights],
        out_specs=tok(d),
        out_shape=jax.ShapeDtypeStruct((n, d), F32),
        compiler_params=_params("parallel"),
        name="ffn",
    )(x1, ca, *weights)


def kernel(x_prompt, x_sample, mem_prompt, cache_k, cache_v, state_ssm_re, state_ssm_im, cache_mem_k, cache_mem_v, page_table, g_mix, w_in, b_gate, lambda_q1, lambda_k1, lambda_q2, lambda_k2, g_subln, ssm_a_re, ssm_a_im, ssm_b_re, ssm_b_im, ssm_c_re, ssm_c_im, ssm_d, ssm_log_dt, w_glu, b_glu, w_proj_a, w_proj_s, w_out, g_cross, g_mem, w_cq, w_mem_kv, w_co, g_ffn, w_ff_in, w_ff_out, g_final):
    assert g_mix.shape[0] == 1, "single-layer trunk"
    b, t, d = x_prompt.shape
    db, dt_len, _ = x_sample.shape
    heads = cache_k.shape[3]
    dk = cache_k.shape[5]
    dv = cache_v.shape[4]
    wa = heads * dv
    g, p, _ = ssm_b_re.shape[1:]
    ws = ssm_d.shape[1]
    n_mem, heads_x, dx = cache_mem_k.shape[2:]
    wx = heads_x * dx
    row2 = lambda v: v.reshape(1, -1)

    w_in0 = w_in[0]
    qs = dk ** -0.5
    w_q, w_k, w_v, w_u = (w_in0[:, :wa] * qs, w_in0[:, wa:2 * wa],
                          w_in0[:, 2 * wa:3 * wa], w_in0[:, 3 * wa:3 * wa + ws])
    w_gate = w_in0[:, 3 * wa + ws:].astype(BF16)
    w_kvu = jnp.concatenate([w_k, w_v, w_u], axis=1).astype(BF16)
    w_qvt = (w_q * LOG2E).T.astype(BF16)
    w_qkvu = jnp.concatenate([w_q, w_k, w_v, w_u], axis=1).astype(BF16)
    lam_vecs = jnp.concatenate([lambda_q1, lambda_k1, lambda_q2, lambda_k2], axis=0)
    slopes_np = [2.0 ** (-8.0 * (h + 1) / heads) for h in range(heads)]
    bd, cd, a_coef = _s5_weights(ssm_a_re[0], ssm_a_im[0], ssm_b_re[0], ssm_b_im[0],
                                 ssm_c_re[0], ssm_c_im[0], ssm_log_dt[0])
    merge_w = (row2(g_mix[0]), w_gate, row2(b_gate[0]), w_glu[0].astype(BF16), row2(b_glu[0]),
               w_proj_a[0].astype(BF16), w_proj_s[0].astype(BF16), w_out[0].astype(BF16),
               row2(g_cross[0]), w_cq[0].astype(BF16))
    ffn_w = (w_co[0].astype(BF16), row2(g_ffn[0]), w_ff_in[0].astype(BF16),
             w_ff_out[0].astype(BF16), row2(g_final))
    xscale = dx ** -0.5

    mkv = _norm_matmul(mem_prompt.reshape(b * n_mem, d), g_mem[0], w_mem_kv[0].astype(BF16), 512)
    mk = mkv[:, :wx].reshape(b, n_mem, wx)
    mv = mkv[:, wx:].reshape(b, n_mem, wx)
    k32, v32, kbf, u_p, qt, vt = _in_proj_prompt(x_prompt, g_mix[0], w_kvu, w_qvt, wa, ws, heads)
    oa_p = _attn_prompt(qt, kbf, vt, lam_vecs, g_subln[0], slopes_np, heads)
    zeros = jnp.zeros((b, g, p), F32)
    ys_p, xf_p = _s5(u_p, _state_pack(zeros, zeros), bd, cd, a_coef, ssm_d[0])
    tm = MERGE_TOKENS
    sub = V7X_SUBLANES
    tok3 = lambda w: pl.BlockSpec((None, tm, w), lambda bi, i: (bi, i, 0))
    ys_spec = pl.BlockSpec((tm // sub, None, sub, ws), lambda bi, i: (i, bi, 0, 0))
    x1_p, qx_p = _merge(x_prompt, oa_p, ys_p, ys_spec, (b, t // tm), tok3, merge_w, xscale, BF16)
    ca_p = _cross(qx_p, mk.reshape(b, n_mem * heads_x, dx), mv.reshape(b, n_mem * heads_x, dx),
                  heads_x, CROSS_TOKENS)
    y_prompt = _ffn(x1_p.reshape(b * t, d), ca_p.reshape(b * t, wx), ffn_w, FFN_TOKENS).reshape(b, t, d)
    srp, sip = _state_unpack(xf_p, g, p)

    n_s = db * dt_len
    q_s, k_s, v_s, u_s = _in_proj_sample(x_sample.reshape(n_s, d), g_mix[0], w_qkvu, wa, ws)
    oa_s = _attn_sample(q_s.reshape(db, dt_len, wa), k_s.reshape(db, dt_len, wa),
                        v_s.reshape(db, dt_len, wa), cache_k[0], cache_v[0], page_table,
                        lam_vecs, g_subln[0], slopes_np, heads)
    assert dt_len == sub, "sample rows (batch, step) are the S5 row order only for 8 new tokens"
    ys_s, xf_s = _s5(u_s.reshape(1, db, sub, ws),
                     _state_pack(state_ssm_re[0], state_ssm_im[0]), bd, cd, a_coef, ssm_d[0])
    ys_s = ys_s.reshape(n_s, ws)
    tok2 = lambda w: pl.BlockSpec((n_s, w), lambda i: (0, 0))
    x1_s, qx_s = _merge(x_sample.reshape(n_s, d), oa_s.reshape(n_s, wa), ys_s, tok2(ws), (1,),
                        tok2, merge_w, xscale, F32)
    ca_s = _cross(qx_s.reshape(db, dt_len, wx), cache_mem_k[0].reshape(db, n_mem * heads_x, dx),
                  cache_mem_v[0].reshape(db, n_mem * heads_x, dx), heads_x, dt_len)
    y_sample = _ffn(x1_s, ca_s.reshape(n_s, wx), ffn_w, FFN_TOKENS).reshape(db, dt_len, d)
    srs, sis = _state_unpack(xf_s, g, p)

    return (y_prompt, y_sample,
            k32.reshape(1, b, t, heads, 2, dk), v32.reshape(1, b, t, heads, dv),
            k_s.reshape(1, db, dt_len, heads, 2, dk), v_s.reshape(1, db, dt_len, heads, dv),
            srp[None], sip[None], srs[None], sis[None],
            mk.reshape(1, b, n_mem, heads_x, dx), mv.reshape(1, b, n_mem, heads_x, dx))
```

```python
import functools
import math

import numpy as np
import jax
import jax.numpy as jnp
from jax import lax
from jax.experimental import pallas as pl
from jax.experimental.pallas import tpu as pltpu

F32 = jnp.float32
BF16 = jnp.bfloat16
EPS = 1e-5
LOG2E = math.log2(math.e)
LAM_INIT = 0.8 - 0.6 * math.exp(-0.3 * 0)

V7X_LANES = 128
V7X_SUBLANES = 8
V7X_VMEM_LIMIT_BYTES = 48 * 1024 * 1024

IN_PROJ_TOKENS = 512
ATTN_Q_BLOCK = 512
ATTN_UNROLL = 4
SAMPLE_PAGES_PER_STEP = 32
S5_CHUNK_ROWS = 512
S5_BATCH_BLOCK = 64
MERGE_TOKENS = 512
FFN_TOKENS = 256
CROSS_TOKENS = 512
CROSS_SAMPLE_BATCHES = 8


def _params(*sem):
    return pltpu.CompilerParams(dimension_semantics=sem,
                                vmem_limit_bytes=V7X_VMEM_LIMIT_BYTES)


def _rmsnorm(x, g):
    r = lax.rsqrt(jnp.mean(x * x, axis=-1, keepdims=True) + EPS)
    return (x * r) * g


def _dot(a, b):
    return jnp.dot(a, b, preferred_element_type=F32)


def _dot_nt(a, b):
    return lax.dot_general(a, b, (((1,), (1,)), ((), ())), preferred_element_type=F32)


def _full(shape):
    zeros = (0,) * len(shape)
    return pl.BlockSpec(shape, lambda *_: zeros)


def _norm_matmul_kernel(x_ref, g_ref, w_ref, o_ref):
    h = _rmsnorm(x_ref[...], g_ref[...]).astype(BF16)
    o_ref[...] = _dot(h, w_ref[...])


def _norm_matmul(x, g, w_bf16, tm):
    n, d = x.shape
    n_out = w_bf16.shape[1]
    return pl.pallas_call(
        _norm_matmul_kernel,
        grid=(n // tm,),
        in_specs=[pl.BlockSpec((tm, d), lambda i: (i, 0)), _full((1, d)), _full((d, n_out))],
        out_specs=pl.BlockSpec((tm, n_out), lambda i: (i, 0)),
        out_shape=jax.ShapeDtypeStruct((n, n_out), F32),
        compiler_params=_params("parallel"),
        name="norm_matmul",
    )(x, g.reshape(1, d), w_bf16)


def _in_proj_prompt_kernel(x_ref, g_ref, wkvu_ref, wqvt_ref,
                           k32_ref, v32_ref, kbf_ref, u_ref, qt_ref, vt_ref, *, wa, heads):
    h = _rmsnorm(x_ref[...], g_ref[...]).astype(BF16)
    r = _dot(h, wkvu_ref[...])
    k = r[:, :wa]
    k32_ref[...] = k
    kbf_ref[...] = k.astype(BF16)
    v = r[:, wa:2 * wa]
    dv = wa // heads
    for hd in range(heads):
        v32_ref[pl.ds(hd, v.shape[0], stride=heads), :] = v[:, hd * dv:(hd + 1) * dv]
    vt_ref[...] = v.T.astype(BF16)
    u_ref[...] = r[:, 2 * wa:].reshape(u_ref.shape)
    qt_ref[...] = _dot_nt(wqvt_ref[...], h).astype(BF16)


def _in_proj_prompt(x, g, wkvu, wqvt, wa, ws, heads):
    b, t, d = x.shape
    tm = IN_PROJ_TOKENS
    sub = V7X_SUBLANES
    row = lambda bi, i: (bi, i, 0)
    col = lambda bi, i: (bi, 0, i)
    return pl.pallas_call(
        functools.partial(_in_proj_prompt_kernel, wa=wa, heads=heads),
        grid=(b, t // tm),
        in_specs=[pl.BlockSpec((None, tm, d), row), _full((1, d)),
                  _full(wkvu.shape), _full(wqvt.shape)],
        out_specs=[pl.BlockSpec((None, tm, wa), row),
                   pl.BlockSpec((None, tm * heads, wa // heads), row),
                   pl.BlockSpec((None, tm, wa), row),
                   pl.BlockSpec((tm // sub, None, sub, ws), lambda bi, i: (i, bi, 0, 0)),
                   pl.BlockSpec((None, wa, tm), col), pl.BlockSpec((None, wa, tm), col)],
        out_shape=[jax.ShapeDtypeStruct((b, t, wa), F32),
                   jax.ShapeDtypeStruct((b, t * heads, wa // heads), F32),
                   jax.ShapeDtypeStruct((b, t, wa), BF16),
                   jax.ShapeDtypeStruct((t // sub, b, sub, ws), F32),
                   jax.ShapeDtypeStruct((b, wa, t), BF16), jax.ShapeDtypeStruct((b, wa, t), BF16)],
        compiler_params=_params("parallel", "parallel"),
        name="in_proj_prompt",
    )(x, g.reshape(1, d), wkvu, wqvt)


def _in_proj_sample_kernel(x_ref, g_ref, w_ref, q_ref, k_ref, v_ref, u_ref, *, wa):
    h = _rmsnorm(x_ref[...], g_ref[...]).astype(BF16)
    r = _dot(h, w_ref[...])
    q_ref[...] = r[:, :wa]
    k_ref[...] = r[:, wa:2 * wa]
    v_ref[...] = r[:, 2 * wa:3 * wa]
    u_ref[...] = r[:, 3 * wa:]


def _in_proj_sample(x, g, w, wa, ws):
    n, d = x.shape
    shapes = [jax.ShapeDtypeStruct((n, wa), F32)] * 3 + [jax.ShapeDtypeStruct((n, ws), F32)]
    return pl.pallas_call(
        functools.partial(_in_proj_sample_kernel, wa=wa),
        grid=(1,),
        in_specs=[_full((n, d)), _full((1, d)), _full(w.shape)],
        out_specs=[_full(s.shape) for s in shapes],
        out_shape=shapes,
        compiler_params=_params("arbitrary"),
        name="in_proj_sample",
    )(x, g.reshape(1, d), w)


def _lambda_full(lam_ref):
    lv = lam_ref[...]
    e1 = jnp.exp(jnp.sum(lv[0:1] * lv[1:2], axis=-1, keepdims=True))
    e2 = jnp.exp(jnp.sum(lv[2:3] * lv[3:4], axis=-1, keepdims=True))
    return e1 - e2 + LAM_INIT


def _attn_prompt_kernel(slopes_ref, qt_ref, k_ref, vt_ref, lam_ref, gsub_ref, o_ref,
                        acc_ref, m_ref, sa_ref, sb_ref, *, blk, dk, ones_rows, unroll):
    i = pl.program_id(2)
    slope = slopes_ref[pl.program_id(1)]
    dv = vt_ref.shape[0]
    qt = qt_ref[...]
    zero = jnp.zeros((dk, blk), BF16)
    qm = (jnp.concatenate([qt[:dk], zero], axis=0), jnp.concatenate([zero, qt[dk:]], axis=0))
    v_ones = jnp.ones((ones_rows, blk), BF16)
    krel = lax.broadcasted_iota(jnp.int32, (blk, blk), 0)
    qrel = lax.broadcasted_iota(jnp.int32, (blk, blk), 1)
    kbias = slope * krel.astype(F32)

    def scores(j):
        k = k_ref[pl.ds(pl.multiple_of(j * blk, blk), blk), :]
        return tuple(_dot(k, q) + kbias for q in qm)

    def fold(s_ref, j, masked):
        off = slope * ((j - i) * blk).astype(F32)
        vt = vt_ref[:, pl.ds(pl.multiple_of(j * blk, blk), blk)]
        vta = jnp.concatenate([vt, v_ones], axis=0)
        for mp in range(2):
            s = s_ref[mp]
            if masked:
                s = jnp.where(krel <= qrel, s, -jnp.inf)
            m = m_ref[mp:mp + 1]
            m_new = jnp.maximum(m, jnp.max(s, axis=0, keepdims=True) + off)
            p = jnp.exp2(s - (m_new - off)).astype(BF16)
            acc_ref[mp] = jnp.exp2(m - m_new) * acc_ref[mp] + _dot(vta, p)
            m_ref[mp:mp + 1] = m_new

    def stage(j, s_cur_ref, s_next_ref):
        for mp, s in enumerate(scores(j + 1)):
            s_next_ref[mp] = s
        fold(s_cur_ref, j, False)

    s_bufs = (sa_ref, sb_ref)

    def unrolled(n, _):
        for u in range(unroll):
            stage(unroll * n + u, s_bufs[u % 2], s_bufs[(u + 1) % 2])
        return 0

    acc_ref[...] = jnp.zeros_like(acc_ref)
    m_ref[...] = jnp.full_like(m_ref, -jnp.inf)
    for mp, s in enumerate(scores(0)):
        sa_ref[mp] = s
    n_full = i // unroll
    lax.fori_loop(0, n_full, unrolled, 0)
    j0 = n_full * unroll
    rest = i - j0
    for width in (w for w in (4, 2) if w < unroll):
        @pl.when(lax.bitwise_and(rest, width) != 0)
        def _():
            start = j0 + lax.bitwise_and(rest, unroll - 2 * width)
            for u in range(width):
                stage(start + u, s_bufs[u % 2], s_bufs[(u + 1) % 2])

    @pl.when(rest % 2 == 1)
    def _():
        stage(i - 1, sa_ref, sb_ref)
        fold(sb_ref, i, True)

    @pl.when(rest % 2 == 0)
    def _():
        fold(sa_ref, i, True)

    lam = _lambda_full(lam_ref)
    a0 = acc_ref[0]
    a1 = acc_ref[1]
    o = a0[:dv] / a0[dv:dv + 1] - lam * (a1[:dv] / a1[dv:dv + 1])
    r = lax.rsqrt(jnp.mean(o * o, axis=0, keepdims=True) + EPS)
    o = (o * r) * gsub_ref[...] * (1.0 - LAM_INIT)
    o_ref[...] = o.T.astype(o_ref.dtype)


def _attn_prompt(qt, kbf, vt, lam_vecs, g_subln, slopes_np, heads):
    b, wa, t = qt.shape
    dv = wa // heads
    dk = dv // 2
    blk = ATTN_Q_BLOCK
    ones_rows = 2 * V7X_SUBLANES
    slopes = jnp.asarray(slopes_np, F32) * LOG2E
    return pl.pallas_call(
        functools.partial(_attn_prompt_kernel, blk=blk, dk=dk, ones_rows=ones_rows,
                          unroll=ATTN_UNROLL),
        grid=(b, heads, t // blk),
        in_specs=[pl.BlockSpec(memory_space=pltpu.SMEM),
                  pl.BlockSpec((None, dv, blk), lambda bi, h, i: (bi, h, i)),
                  pl.BlockSpec((None, t, dv), lambda bi, h, i: (bi, 0, h)),
                  pl.BlockSpec((None, dv, t), lambda bi, h, i: (bi, h, 0)),
                  _full(lam_vecs.shape), _full((dv, 1))],
        out_specs=pl.BlockSpec((None, blk, dv), lambda bi, h, i: (bi, i, h)),
        out_shape=jax.ShapeDtypeStruct((b, t, wa), BF16),
        scratch_shapes=[pltpu.VMEM((2, dv + ones_rows, blk), F32), pltpu.VMEM((2, blk), F32),
                        pltpu.VMEM((2, blk, blk), F32), pltpu.VMEM((2, blk, blk), F32)],
        compiler_params=_params("parallel", "parallel", "arbitrary"),
        name="attn_prompt",
    )(slopes, qt, kbf, vt, lam_vecs, g_subln.reshape(dv, 1))


def _attn_sample_kernel(pt_ref, *refs, pps, heads, tq, past_len, page):
    k_pages = refs[:pps]
    v_pages = refs[pps:2 * pps]
    (q_ref, kn_ref, vn_ref, slope_ref, qidx_ref, qmask_ref, lam_ref, gsub_ref, o_ref,
     qbd_ref, kpad_ref, vpad_ref, m_ref, l_ref, acc_ref) = refs[2 * pps:]
    c = pl.program_id(1)
    hr = 2 * tq
    rows = heads * hr
    dv = q_ref.shape[-1] // heads

    slope = slope_ref[...]
    qidx = qidx_ref[...]
    lane = lax.broadcasted_iota(jnp.int32, (rows, page), 1)

    def fold(s, head_vals):
        m_old = m_ref[...]
        m_new = jnp.maximum(m_old, jnp.max(s, axis=-1, keepdims=True))
        p = jnp.exp(s - m_new)
        alpha = jnp.exp(m_old - m_new)
        l_ref[...] = alpha * l_ref[...] + jnp.sum(p, axis=-1, keepdims=True)
        pb = p.astype(BF16)
        for h in range(heads):
            r = slice(h * hr, (h + 1) * hr)
            pv = _dot(pb[r, :page], head_vals(0, h))
            for j in range(1, s.shape[1] // page):
                pv += _dot(pb[r, j * page:(j + 1) * page], head_vals(j, h))
            acc_ref[r, :] = alpha[r] * acc_ref[r, :] + pv
        m_ref[...] = m_new

    @pl.when(c == 0)
    def _():
        q = q_ref[...]
        qt = jnp.concatenate([q] * (rows // tq), axis=0)
        qbd_ref[...] = (qt * qmask_ref[...]).astype(BF16)
        m_ref[...] = jnp.full_like(m_ref, -jnp.inf)
        l_ref[...] = jnp.zeros_like(l_ref)
        acc_ref[...] = jnp.zeros_like(acc_ref)
        kpad_ref[...] = jnp.zeros_like(kpad_ref)
        vpad_ref[...] = jnp.zeros_like(vpad_ref)
        kpad_ref[0:tq, :] = kn_ref[...]
        vpad_ref[0:tq, :] = vn_ref[...]
        s = _dot_nt(qbd_ref[...], kpad_ref[...].astype(BF16))
        s = s - slope * (qidx - lane).astype(F32)
        s = jnp.where(lane <= qidx, s, -jnp.inf)
        fold(s, lambda j, h: vpad_ref[:, h * dv:(h + 1) * dv].astype(BF16))

    qbd = qbd_ref[...]
    qpos = past_len + qidx
    parts = []
    for j in range(pps):
        s = _dot(qbd, k_pages[j][...].astype(BF16))
        kpos = (c * pps + j) * page + lane
        parts.append(s - slope * (qpos - kpos).astype(F32))
    fold(jnp.concatenate(parts, axis=1),
         lambda j, h: v_pages[j][pl.ds(h, page, stride=heads), :].astype(BF16))

    @pl.when(c == pl.num_programs(1) - 1)
    def _():
        lam = _lambda_full(lam_ref)
        o = acc_ref[...] / l_ref[...]
        for h in range(heads):
            d = o[h * hr:h * hr + tq] - lam * o[h * hr + tq:(h + 1) * hr]
            d = _rmsnorm(d, gsub_ref[...]) * (1.0 - LAM_INIT)
            o_ref[:, h * dv:(h + 1) * dv] = d.astype(o_ref.dtype)


def _attn_sample(q, k_new, v_new, pool_k, pool_v, page_table, lam_vecs, g_subln, slopes_np, heads):
    db, tq, wa = q.shape
    n_phys, page = pool_k.shape[0], pool_k.shape[1]
    n_pages = page_table.shape[1]
    past_len = n_pages * page
    pps = math.gcd(SAMPLE_PAGES_PER_STEP, n_pages)
    dv = wa // heads
    dk = dv // 2
    rows = heads * 2 * tq
    pk = jnp.transpose(pool_k, (0, 2, 3, 4, 1)).reshape(n_phys, wa, page)
    pv = pool_v.reshape(n_phys, page * heads, dv)
    r_idx = np.arange(rows)
    slope_rows = jnp.asarray(np.repeat(np.asarray(slopes_np)[r_idx // (2 * tq)][:, None], page, 1), F32)
    qidx_rows = jnp.asarray(np.repeat((r_idx % tq)[:, None], page, 1), jnp.int32)
    qmask = jnp.asarray((r_idx // tq)[:, None] == (np.arange(wa) // dk)[None, :], F32)

    def page_spec(j, shape):
        return pl.BlockSpec((None,) + shape,
                            lambda b, c, pt: (pt[b * n_pages + c * pps + j], 0, 0))

    per_b = lambda b, c, pt: (b, 0, 0)
    const2 = lambda b, c, pt: (0, 0)
    grid_spec = pltpu.PrefetchScalarGridSpec(
        num_scalar_prefetch=1,
        grid=(db, n_pages // pps),
        in_specs=([page_spec(j, (wa, page)) for j in range(pps)]
                  + [page_spec(j, (page * heads, dv)) for j in range(pps)]
                  + [pl.BlockSpec((None, tq, wa), per_b)] * 3
                  + [pl.BlockSpec((rows, page), const2), pl.BlockSpec((rows, page), const2),
                     pl.BlockSpec((rows, wa), const2),
                     pl.BlockSpec(lam_vecs.shape, const2),
                     pl.BlockSpec((1, dv), const2)]),
        out_specs=pl.BlockSpec((None, tq, wa), per_b),
        scratch_shapes=[pltpu.VMEM((rows, wa), BF16),
                        pltpu.VMEM((page, wa), F32), pltpu.VMEM((page, wa), F32),
                        pltpu.VMEM((rows, 1), F32), pltpu.VMEM((rows, 1), F32),
                        pltpu.VMEM((rows, dv), F32)])
    return pl.pallas_call(
        functools.partial(_attn_sample_kernel, pps=pps, heads=heads, tq=tq,
                          past_len=past_len, page=page),
        grid_spec=grid_spec,
        out_shape=jax.ShapeDtypeStruct((db, tq, wa), F32),
        compiler_params=_params("parallel", "arbitrary"),
        name="attn_sample",
    )(page_table.reshape(-1), *([pk] * pps), *([pv] * pps), q, k_new, v_new,
      slope_rows, qidx_rows, qmask, lam_vecs, g_subln.reshape(1, dv))


def _s5_kernel(u_ref, x0_ref, bd_ref, cd_ref, a_ref, d_ref, y_ref, xf_ref,
               us_ref, xs_ref, ys_ref, st_ref, *, nbb, tbs, half_in, half_state):
    c = pl.program_id(1)

    @pl.when(c == 0)
    def _():
        st_ref[...] = x0_ref[...]

    sub, lanes = V7X_SUBLANES, V7X_LANES
    width = u_ref.shape[-1]
    n_us = width // lanes
    hs = half_state

    def batch_rows(tb, bb):
        return pl.ds(tb * sub * nbb + bb, sub, stride=nbb)

    for tb in range(tbs):
        for bb in range(nbb):
            ut = u_ref[tb, bb]
            for s in range(n_us):
                us_ref[s, batch_rows(tb, bb), :] = ut[:, s * lanes:(s + 1) * lanes]
    ub = jnp.concatenate([us_ref[s] for s in range(n_us)], axis=1).astype(BF16)
    for k in range(2):
        xs_ref[:, 2 * k * hs:2 * (k + 1) * hs] = _dot(
            ub[:, k * half_in:(k + 1) * half_in], bd_ref[k])

    for k in range(2):
        re0, im0 = 2 * k * hs, (2 * k + 1) * hs
        a_re = a_ref[2 * k]
        a_im = a_ref[2 * k + 1]

        def group(r, _):
            rows0 = pl.multiple_of(r * sub, sub)
            x_re = st_ref[pl.ds(rows0, sub), re0:re0 + hs]
            x_im = st_ref[pl.ds(rows0, sub), im0:im0 + hs]

            def step(t, carry):
                x_re, x_im = carry
                row = pl.multiple_of(t * nbb + rows0, sub)
                n_re = a_re * x_re - a_im * x_im + xs_ref[pl.ds(row, sub), re0:re0 + hs]
                n_im = a_re * x_im + a_im * x_re + xs_ref[pl.ds(row, sub), im0:im0 + hs]
                xs_ref[pl.ds(row, sub), re0:re0 + hs] = n_re
                xs_ref[pl.ds(row, sub), im0:im0 + hs] = n_im
                return n_re, n_im

            x_re, x_im = lax.fori_loop(0, tbs * sub, step, (x_re, x_im))
            st_ref[pl.ds(rows0, sub), re0:re0 + hs] = x_re
            st_ref[pl.ds(rows0, sub), im0:im0 + hs] = x_im
            return 0

        lax.fori_loop(0, nbb // sub, group, 0)

    for k in range(2):
        yk = _dot(xs_ref[:, 2 * k * hs:2 * (k + 1) * hs].astype(BF16), cd_ref[k])
        for s in range(yk.shape[1] // lanes):
            ys_ref[k * (yk.shape[1] // lanes) + s] = yk[:, s * lanes:(s + 1) * lanes]
    d = d_ref[...]
    for tb in range(tbs):
        for bb in range(nbb):
            yt = jnp.concatenate([ys_ref[s, batch_rows(tb, bb), :] for s in range(n_us)], axis=1)
            y_ref[tb, bb] = yt + d * u_ref[tb, bb]

    @pl.when(c == pl.num_programs(1) - 1)
    def _():
        xf_ref[...] = st_ref[...]


def _s5(u4, x0, bd, cd, a_coef, d_vec):
    n_tb, nb, sub, width = u4.shape
    nbb = min(nb, S5_BATCH_BLOCK)
    tbs = min(n_tb, max(1, S5_CHUNK_ROWS // (nbb * sub)))
    rows = tbs * nbb * sub
    n_state = x0.shape[1]
    tok = pl.BlockSpec((tbs, nbb, sub, width), lambda g, c: (c, g, 0, 0))
    st = pl.BlockSpec((nbb, n_state), lambda g, c: (g, 0))
    slabs = pltpu.VMEM((width // V7X_LANES, rows, V7X_LANES), F32)
    return pl.pallas_call(
        functools.partial(_s5_kernel, nbb=nbb, tbs=tbs, half_in=width // 2,
                          half_state=n_state // 4),
        grid=(nb // nbb, n_tb // tbs),
        in_specs=[tok, st, _full(bd.shape), _full(cd.shape), _full(a_coef.shape),
                  _full((1, width))],
        out_specs=[tok, st],
        out_shape=[jax.ShapeDtypeStruct(u4.shape, F32), jax.ShapeDtypeStruct(x0.shape, F32)],
        scratch_shapes=[slabs, pltpu.VMEM((rows, n_state), F32), slabs,
                        pltpu.VMEM((nbb, n_state), F32)],
        compiler_params=_params("parallel", "arbitrary"),
        name="s5_scan",
    )(u4, x0, bd, cd, a_coef, d_vec.reshape(1, width))


def _s5_weights(a_re, a_im, b_re, b_im, c_re, c_im, log_dt):
    g, p, ch = b_re.shape
    dt = jnp.exp(log_dt)[:, None]
    mag = jnp.exp(dt * a_re)
    abar_re, abar_im = mag * jnp.cos(dt * a_im), mag * jnp.sin(dt * a_im)
    den = a_re * a_re + a_im * a_im
    f_re = ((abar_re - 1.0) * a_re + abar_im * a_im) / den
    f_im = (abar_im * a_re - (abar_re - 1.0) * a_im) / den
    bbar_re = f_re[..., None] * b_re - f_im[..., None] * b_im
    bbar_im = f_re[..., None] * b_im + f_im[..., None] * b_re
    gh = g // 2
    eye = jnp.eye(gh, dtype=F32)

    def bd_half(w):
        return jnp.einsum('gpc,gh->gchp', w, eye).reshape(gh * ch, gh * p)

    def cd_half(w):
        return jnp.einsum('gcp,gh->gphc', w, eye).reshape(gh * p, gh * ch)

    bd = jnp.stack([jnp.concatenate([bd_half(bbar_re[k * gh:(k + 1) * gh]),
                                     bd_half(bbar_im[k * gh:(k + 1) * gh])], axis=1)
                    for k in range(2)]).astype(BF16)
    cd = jnp.stack([jnp.concatenate([cd_half(c_re[k * gh:(k + 1) * gh]),
                                     cd_half(-c_im[k * gh:(k + 1) * gh])], axis=0)
                    for k in range(2)]).astype(BF16)
    coef = [w[k * gh:(k + 1) * gh].reshape(1, gh * p) for k in range(2) for w in (abar_re, abar_im)]
    a_coef = jnp.broadcast_to(jnp.stack(coef), (4, V7X_SUBLANES, gh * p))
    return bd, cd, a_coef


def _state_pack(x_re, x_im):
    nb, g, p = x_re.shape
    hs = g * p // 2
    re = x_re.reshape(nb, 2, hs)
    im = x_im.reshape(nb, 2, hs)
    return jnp.stack([re[:, 0], im[:, 0], re[:, 1], im[:, 1]], axis=1).reshape(nb, 4 * hs)


def _state_unpack(x, g, p):
    nb = x.shape[0]
    hs = g * p // 2
    x = x.reshape(nb, 4, hs)
    re = jnp.stack([x[:, 0], x[:, 2]], axis=1).reshape(nb, g, p)
    im = jnp.stack([x[:, 1], x[:, 3]], axis=1).reshape(nb, g, p)
    return re, im


def _merge_kernel(x_ref, oa_ref, ys_ref, gmix_ref, wg_ref, bg_ref, wglu_ref, bglu_ref,
                  wpa_ref, wps_ref, wout_ref, gcross_ref, wcq_ref, x1_ref, qx_ref, *, qscale):
    x = x_ref[...]
    d = x.shape[-1]
    h = _rmsnorm(x, gmix_ref[...]).astype(BF16)
    gate = jax.nn.sigmoid(_dot(h, wg_ref[...]) + bg_ref[...])
    z = jax.nn.gelu(ys_ref[...].reshape(x.shape[0], -1))
    o_s = z * jax.nn.sigmoid(_dot(z.astype(BF16), wglu_ref[...]) + bglu_ref[...])
    pa = _dot(oa_ref[...].astype(BF16), wpa_ref[...])
    ps = _dot(o_s.astype(BF16), wps_ref[...])
    mix = gate[:, :d] * pa + gate[:, d:] * ps
    x1 = x + _dot(mix.astype(BF16), wout_ref[...])
    x1_ref[...] = x1
    hc = _rmsnorm(x1, gcross_ref[...]).astype(BF16)
    qx_ref[...] = (_dot(hc, wcq_ref[...]) * qscale).astype(qx_ref.dtype)


def _merge(x, oa, ys, ys_spec, grid, tok_spec, weights, qscale, inter_dtype):
    (gmix, wg, bg, wglu, bglu, wpa, wps, wout, gcross, wcq) = weights
    d = x.shape[-1]
    wx = wcq.shape[1]
    wspecs = [_full(w.shape) for w in weights]
    return pl.pallas_call(
        functools.partial(_merge_kernel, qscale=qscale),
        grid=grid,
        in_specs=[tok_spec(d), tok_spec(oa.shape[-1]), ys_spec] + wspecs,
        out_specs=[tok_spec(d), tok_spec(wx)],
        out_shape=[jax.ShapeDtypeStruct(x.shape, F32),
                   jax.ShapeDtypeStruct(x.shape[:-1] + (wx,), inter_dtype)],
        compiler_params=_params(*(("parallel",) * len(grid))),
        name="merge",
    )(x, oa, ys, *weights)


def _cross_kernel(q_ref, k_ref, v_ref, o_ref, *, heads):
    dx = q_ref.shape[-1] // heads
    n_mem = k_ref.shape[1] // heads
    for bb in range(q_ref.shape[0]):
        for h in range(heads):
            sl = slice(h * dx, (h + 1) * dx)
            rows = pl.ds(h, n_mem, stride=heads)
            q = q_ref[bb, :, sl].astype(BF16)
            s = _dot_nt(q, k_ref[bb, rows, :].astype(BF16))
            m = jnp.max(s, axis=-1, keepdims=True)
            p = jnp.exp(s - m)
            l = jnp.sum(p, axis=-1, keepdims=True)
            o = _dot(p.astype(BF16), v_ref[bb, rows, :].astype(BF16)) / l
            o_ref[bb, :, sl] = o.astype(o_ref.dtype)


def _cross(qx, mem_k, mem_v, heads, tm, bblk):
    b, t, wx = qx.shape
    rows, dx = mem_k.shape[1:]
    tok = pl.BlockSpec((bblk, tm, wx), lambda bi, i: (bi, i, 0))
    mem = pl.BlockSpec((bblk, rows, dx), lambda bi, i: (bi, 0, 0))
    return pl.pallas_call(
        functools.partial(_cross_kernel, heads=heads),
        grid=(b // bblk, t // tm),
        in_specs=[tok, mem, mem],
        out_specs=tok,
        out_shape=jax.ShapeDtypeStruct(qx.shape, qx.dtype),
        compiler_params=_params("parallel", "parallel"),
        name="cross_attn",
    )(qx, mem_k, mem_v)


def _ffn_kernel(x_ref, ca_ref, wco_ref, gffn_ref, wfi_ref, wfo_ref, gfin_ref, y_ref, *, dff):
    x2 = x_ref[...] + _dot(ca_ref[...].astype(BF16), wco_ref[...])
    h = _rmsnorm(x2, gffn_ref[...]).astype(BF16)
    ff = _dot(h, wfi_ref[...])
    hid = jax.nn.silu(ff[:, :dff]) * ff[:, dff:]
    x3 = x2 + _dot(hid.astype(BF16), wfo_ref[...])
    y_ref[...] = _rmsnorm(x3, gfin_ref[...])


def _ffn(x1, ca, weights, tm):
    (wco, gffn, wfi, wfo, gfin) = weights
    n, d = x1.shape
    tm = min(tm, n)
    tok = lambda w: pl.BlockSpec((tm, w), lambda i: (i, 0))
    return pl.pallas_call(
        functools.partial(_ffn_kernel, dff=wfo.shape[0]),
        grid=(n // tm,),
        in_specs=[tok(d), tok(ca.shape[-1])] + [_full(w.shape) for w in weights],
        out_specs=tok(d),
        out_shape=jax.ShapeDtypeStruct((n, d), F32),
        compiler_params=_params("parallel"),
        name="ffn",
    )(x1, ca, *weights)


def kernel(x_prompt, x_sample, mem_prompt, cache_k, cache_v, state_ssm_re, state_ssm_im, cache_mem_k, cache_mem_v, page_table, g_mix, w_in, b_gate, lambda_q1, lambda_k1, lambda_q2, lambda_k2, g_subln, ssm_a_re, ssm_a_im, ssm_b_re, ssm_b_im, ssm_c_re, ssm_c_im, ssm_d, ssm_log_dt, w_glu, b_glu, w_proj_a, w_proj_s, w_out, g_cross, g_mem, w_cq, w_mem_kv, w_co, g_ffn, w_ff_in, w_ff_out, g_final):
    assert g_mix.shape[0] == 1, "single-layer trunk"
    b, t, d = x_prompt.shape
    db, dt_len, _ = x_sample.shape
    heads = cache_k.shape[3]
    dk = cache_k.shape[5]
    dv = cache_v.shape[4]
    wa = heads * dv
    g, p, _ = ssm_b_re.shape[1:]
    ws = ssm_d.shape[1]
    n_mem, heads_x, dx = cache_mem_k.shape[2:]
    wx = heads_x * dx
    row2 = lambda v: v.reshape(1, -1)

    w_in0 = w_in[0]
    qs = dk ** -0.5
    w_q, w_k, w_v, w_u = (w_in0[:, :wa] * qs, w_in0[:, wa:2 * wa],
                          w_in0[:, 2 * wa:3 * wa], w_in0[:, 3 * wa:3 * wa + ws])
    w_gate = w_in0[:, 3 * wa + ws:].astype(BF16)
    w_kvu = jnp.concatenate([w_k, w_v, w_u], axis=1).astype(BF16)
    w_qvt = (w_q * LOG2E).T.astype(BF16)
    w_qkvu = jnp.concatenate([w_q, w_k, w_v, w_u], axis=1).astype(BF16)
    lam_vecs = jnp.concatenate([lambda_q1, lambda_k1, lambda_q2, lambda_k2], axis=0)
    slopes_np = [2.0 ** (-8.0 * (h + 1) / heads) for h in range(heads)]
    bd, cd, a_coef = _s5_weights(ssm_a_re[0], ssm_a_im[0], ssm_b_re[0], ssm_b_im[0],
                                 ssm_c_re[0], ssm_c_im[0], ssm_log_dt[0])
    merge_w = (row2(g_mix[0]), w_gate, row2(b_gate[0]), w_glu[0].astype(BF16), row2(b_glu[0]),
               w_proj_a[0].astype(BF16), w_proj_s[0].astype(BF16), w_out[0].astype(BF16),
               row2(g_cross[0]), w_cq[0].astype(BF16))
    ffn_w = (w_co[0].astype(BF16), row2(g_ffn[0]), w_ff_in[0].astype(BF16),
             w_ff_out[0].astype(BF16), row2(g_final))
    xscale = dx ** -0.5

    mkv = _norm_matmul(mem_prompt.reshape(b * n_mem, d), g_mem[0], w_mem_kv[0].astype(BF16), 512)
    mk = mkv[:, :wx].reshape(b, n_mem, wx)
    mv = mkv[:, wx:].reshape(b, n_mem, wx)
    k32, v32, kbf, u_p, qt, vt = _in_proj_prompt(x_prompt, g_mix[0], w_kvu, w_qvt, wa, ws, heads)
    oa_p = _attn_prompt(qt, kbf, vt, lam_vecs, g_subln[0], slopes_np, heads)
    zeros = jnp.zeros((b, g, p), F32)
    ys_p, xf_p = _s5(u_p, _state_pack(zeros, zeros), bd, cd, a_coef, ssm_d[0])
    tm = MERGE_TOKENS
    sub = V7X_SUBLANES
    tok3 = lambda w: pl.BlockSpec((None, tm, w), lambda bi, i: (bi, i, 0))
    ys_spec = pl.BlockSpec((tm // sub, None, sub, ws), lambda bi, i: (i, bi, 0, 0))
    x1_p, qx_p = _merge(x_prompt, oa_p, ys_p, ys_spec, (b, t // tm), tok3, merge_w, xscale, BF16)
    ca_p = _cross(qx_p, mk.reshape(b, n_mem * heads_x, dx), mv.reshape(b, n_mem * heads_x, dx),
                  heads_x, CROSS_TOKENS, 1)
    y_prompt = _ffn(x1_p.reshape(b * t, d), ca_p.reshape(b * t, wx), ffn_w, FFN_TOKENS).reshape(b, t, d)
    srp, sip = _state_unpack(xf_p, g, p)

    n_s = db * dt_len
    q_s, k_s, v_s, u_s = _in_proj_sample(x_sample.reshape(n_s, d), g_mix[0], w_qkvu, wa, ws)
    oa_s = _attn_sample(q_s.reshape(db, dt_len, wa), k_s.reshape(db, dt_len, wa),
                        v_s.reshape(db, dt_len, wa), cache_k[0], cache_v[0], page_table,
                        lam_vecs, g_subln[0], slopes_np, heads)
    assert dt_len == sub, "sample rows (batch, step) are the S5 row order only for 8 new tokens"
    ys_s, xf_s = _s5(u_s.reshape(1, db, sub, ws),
                     _state_pack(state_ssm_re[0], state_ssm_im[0]), bd, cd, a_coef, ssm_d[0])
    ys_s = ys_s.reshape(n_s, ws)
    tok2 = lambda w: pl.BlockSpec((n_s, w), lambda i: (0, 0))
    x1_s, qx_s = _merge(x_sample.reshape(n_s, d), oa_s.reshape(n_s, wa), ys_s, tok2(ws), (1,),
                        tok2, merge_w, xscale, F32)
    ca_s = _cross(qx_s.reshape(db, dt_len, wx), cache_mem_k[0].reshape(db, n_mem * heads_x, dx),
                  cache_mem_v[0].reshape(db, n_mem * heads_x, dx), heads_x, dt_len,
                  math.gcd(db, CROSS_SAMPLE_BATCHES))
    y_sample = _ffn(x1_s, ca_s.reshape(n_s, wx), ffn_w, FFN_TOKENS).reshape(db, dt_len, d)
    srs, sis = _state_unpack(xf_s, g, p)

    return (y_prompt, y_sample,
            k32.reshape(1, b, t, heads, 2, dk), v32.reshape(1, b, t, heads, dv),
            k_s.reshape(1, db, dt_len, heads, 2, dk), v_s.reshape(1, db, dt_len, heads, dv),
            srp[None], sip[None], srs[None], sis[None],
            mk.reshape(1, b, n_mem, heads_x, dx), mv.reshape(1, b, n_mem, heads_x, dx))
```

```python
import functools
import math

import numpy as np
import jax
import jax.numpy as jnp
from jax import lax
from jax.experimental import pallas as pl
from jax.experimental.pallas import tpu as pltpu

F32 = jnp.float32
BF16 = jnp.bfloat16
EPS = 1e-5
LOG2E = math.log2(math.e)
LAM_INIT = 0.8 - 0.6 * math.exp(-0.3 * 0)

V7X_LANES = 128
V7X_SUBLANES = 8
V7X_VMEM_LIMIT_BYTES = 48 * 1024 * 1024

IN_PROJ_TOKENS = 512
ATTN_Q_BLOCK = 512
ATTN_UNROLL = 4
SAMPLE_PAGES_PER_STEP = 32
S5_CHUNK_ROWS = 512
S5_BATCH_BLOCK = 64
MERGE_TOKENS = 512
FFN_TOKENS = 512
CROSS_TOKENS = 512
CROSS_SAMPLE_BATCHES = 8


def _params(*sem):
    return pltpu.CompilerParams(dimension_semantics=sem,
                                vmem_limit_bytes=V7X_VMEM_LIMIT_BYTES)


def _rmsnorm(x, g):
    r = lax.rsqrt(jnp.mean(x * x, axis=-1, keepdims=True) + EPS)
    return (x * r) * g


def _dot(a, b):
    return jnp.dot(a, b, preferred_element_type=F32)


def _dot_nt(a, b):
    return lax.dot_general(a, b, (((1,), (1,)), ((), ())), preferred_element_type=F32)


def _full(shape):
    zeros = (0,) * len(shape)
    return pl.BlockSpec(shape, lambda *_: zeros, pipeline_mode=pl.Buffered(1))


def _norm_matmul_kernel(x_ref, g_ref, w_ref, o_ref):
    h = _rmsnorm(x_ref[...], g_ref[...]).astype(BF16)
    o_ref[...] = _dot(h, w_ref[...])


def _norm_matmul(x, g, w_bf16, tm):
    n, d = x.shape
    n_out = w_bf16.shape[1]
    return pl.pallas_call(
        _norm_matmul_kernel,
        grid=(n // tm,),
        in_specs=[pl.BlockSpec((tm, d), lambda i: (i, 0)), _full((1, d)), _full((d, n_out))],
        out_specs=pl.BlockSpec((tm, n_out), lambda i: (i, 0)),
        out_shape=jax.ShapeDtypeStruct((n, n_out), F32),
        compiler_params=_params("parallel"),
        name="norm_matmul",
    )(x, g.reshape(1, d), w_bf16)


def _in_proj_prompt_kernel(x_ref, g_ref, wkvu_ref, wqvt_ref,
                           k32_ref, v32_ref, kbf_ref, u_ref, qt_ref, vt_ref, *, wa, heads):
    h = _rmsnorm(x_ref[...], g_ref[...]).astype(BF16)
    r = _dot(h, wkvu_ref[...])
    k = r[:, :wa]
    k32_ref[...] = k
    kbf_ref[...] = k.astype(BF16)
    v = r[:, wa:2 * wa]
    dv = wa // heads
    for hd in range(heads):
        v32_ref[pl.ds(hd, v.shape[0], stride=heads), :] = v[:, hd * dv:(hd + 1) * dv]
    vt_ref[...] = v.T.astype(BF16)
    u_ref[...] = r[:, 2 * wa:].reshape(u_ref.shape)
    qt_ref[...] = _dot_nt(wqvt_ref[...], h).astype(BF16)


def _in_proj_prompt(x, g, wkvu, wqvt, wa, ws, heads):
    b, t, d = x.shape
    tm = IN_PROJ_TOKENS
    sub = V7X_SUBLANES
    row = lambda bi, i: (bi, i, 0)
    col = lambda bi, i: (bi, 0, i)
    return pl.pallas_call(
        functools.partial(_in_proj_prompt_kernel, wa=wa, heads=heads),
        grid=(b, t // tm),
        in_specs=[pl.BlockSpec((None, tm, d), row), _full((1, d)),
                  _full(wkvu.shape), _full(wqvt.shape)],
        out_specs=[pl.BlockSpec((None, tm, wa), row),
                   pl.BlockSpec((None, tm * heads, wa // heads), row),
                   pl.BlockSpec((None, tm, wa), row),
                   pl.BlockSpec((tm // sub, None, sub, ws), lambda bi, i: (i, bi, 0, 0)),
                   pl.BlockSpec((None, wa, tm), col), pl.BlockSpec((None, wa, tm), col)],
        out_shape=[jax.ShapeDtypeStruct((b, t, wa), F32),
                   jax.ShapeDtypeStruct((b, t * heads, wa // heads), F32),
                   jax.ShapeDtypeStruct((b, t, wa), BF16),
                   jax.ShapeDtypeStruct((t // sub, b, sub, ws), F32),
                   jax.ShapeDtypeStruct((b, wa, t), BF16), jax.ShapeDtypeStruct((b, wa, t), BF16)],
        compiler_params=_params("parallel", "parallel"),
        name="in_proj_prompt",
    )(x, g.reshape(1, d), wkvu, wqvt)


def _in_proj_sample_kernel(x_ref, g_ref, w_ref, q_ref, k_ref, v_ref, u_ref, *, wa):
    h = _rmsnorm(x_ref[...], g_ref[...]).astype(BF16)
    r = _dot(h, w_ref[...])
    q_ref[...] = r[:, :wa]
    k_ref[...] = r[:, wa:2 * wa]
    v_ref[...] = r[:, 2 * wa:3 * wa]
    u_ref[...] = r[:, 3 * wa:]


def _in_proj_sample(x, g, w, wa, ws):
    n, d = x.shape
    shapes = [jax.ShapeDtypeStruct((n, wa), F32)] * 3 + [jax.ShapeDtypeStruct((n, ws), F32)]
    return pl.pallas_call(
        functools.partial(_in_proj_sample_kernel, wa=wa),
        grid=(1,),
        in_specs=[_full((n, d)), _full((1, d)), _full(w.shape)],
        out_specs=[_full(s.shape) for s in shapes],
        out_shape=shapes,
        compiler_params=_params("arbitrary"),
        name="in_proj_sample",
    )(x, g.reshape(1, d), w)


def _lambda_full(lam_ref):
    lv = lam_ref[...]
    e1 = jnp.exp(jnp.sum(lv[0:1] * lv[1:2], axis=-1, keepdims=True))
    e2 = jnp.exp(jnp.sum(lv[2:3] * lv[3:4], axis=-1, keepdims=True))
    return e1 - e2 + LAM_INIT


def _attn_prompt_kernel(slopes_ref, qt_ref, k_ref, vt_ref, lam_ref, gsub_ref, o_ref,
                        acc_ref, m_ref, sa_ref, sb_ref, *, blk, dk, ones_rows, unroll):
    i = pl.program_id(2)
    slope = slopes_ref[pl.program_id(1)]
    dv = vt_ref.shape[0]
    qt = qt_ref[...]
    zero = jnp.zeros((dk, blk), BF16)
    qm = (jnp.concatenate([qt[:dk], zero], axis=0), jnp.concatenate([zero, qt[dk:]], axis=0))
    v_ones = jnp.ones((ones_rows, blk), BF16)
    krel = lax.broadcasted_iota(jnp.int32, (blk, blk), 0)
    qrel = lax.broadcasted_iota(jnp.int32, (blk, blk), 1)
    kbias = slope * krel.astype(F32)

    def scores(j):
        k = k_ref[pl.ds(pl.multiple_of(j * blk, blk), blk), :]
        return tuple(_dot(k, q) + kbias for q in qm)

    def fold(s_ref, j, masked):
        off = slope * ((j - i) * blk).astype(F32)
        vt = vt_ref[:, pl.ds(pl.multiple_of(j * blk, blk), blk)]
        vta = jnp.concatenate([vt, v_ones], axis=0)
        for mp in range(2):
            s = s_ref[mp]
            if masked:
                s = jnp.where(krel <= qrel, s, -jnp.inf)
            m = m_ref[mp:mp + 1]
            m_new = jnp.maximum(m, jnp.max(s, axis=0, keepdims=True) + off)
            p = jnp.exp2(s - (m_new - off)).astype(BF16)
            acc_ref[mp] = jnp.exp2(m - m_new) * acc_ref[mp] + _dot(vta, p)
            m_ref[mp:mp + 1] = m_new

    def stage(j, s_cur_ref, s_next_ref):
        for mp, s in enumerate(scores(j + 1)):
            s_next_ref[mp] = s
        fold(s_cur_ref, j, False)

    s_bufs = (sa_ref, sb_ref)

    def unrolled(n, _):
        for u in range(unroll):
            stage(unroll * n + u, s_bufs[u % 2], s_bufs[(u + 1) % 2])
        return 0

    acc_ref[...] = jnp.zeros_like(acc_ref)
    m_ref[...] = jnp.full_like(m_ref, -jnp.inf)
    for mp, s in enumerate(scores(0)):
        sa_ref[mp] = s
    n_full = i // unroll
    lax.fori_loop(0, n_full, unrolled, 0)
    j0 = n_full * unroll
    rest = i - j0
    for width in (w for w in (4, 2) if w < unroll):
        @pl.when(lax.bitwise_and(rest, width) != 0)
        def _():
            start = j0 + lax.bitwise_and(rest, unroll - 2 * width)
            for u in range(width):
                stage(start + u, s_bufs[u % 2], s_bufs[(u + 1) % 2])

    @pl.when(rest % 2 == 1)
    def _():
        stage(i - 1, sa_ref, sb_ref)
        fold(sb_ref, i, True)

    @pl.when(rest % 2 == 0)
    def _():
        fold(sa_ref, i, True)

    lam = _lambda_full(lam_ref)
    a0 = acc_ref[0]
    a1 = acc_ref[1]
    o = a0[:dv] / a0[dv:dv + 1] - lam * (a1[:dv] / a1[dv:dv + 1])
    r = lax.rsqrt(jnp.mean(o * o, axis=0, keepdims=True) + EPS)
    o = (o * r) * gsub_ref[...] * (1.0 - LAM_INIT)
    o_ref[...] = o.T.astype(o_ref.dtype)


def _attn_prompt(qt, kbf, vt, lam_vecs, g_subln, slopes_np, heads):
    b, wa, t = qt.shape
    dv = wa // heads
    dk = dv // 2
    blk = ATTN_Q_BLOCK
    ones_rows = 2 * V7X_SUBLANES
    slopes = jnp.asarray(slopes_np, F32) * LOG2E
    return pl.pallas_call(
        functools.partial(_attn_prompt_kernel, blk=blk, dk=dk, ones_rows=ones_rows,
                          unroll=ATTN_UNROLL),
        grid=(b, heads, t // blk),
        in_specs=[pl.BlockSpec(memory_space=pltpu.SMEM),
                  pl.BlockSpec((None, dv, blk), lambda bi, h, i: (bi, h, i)),
                  pl.BlockSpec((None, t, dv), lambda bi, h, i: (bi, 0, h)),
                  pl.BlockSpec((None, dv, t), lambda bi, h, i: (bi, h, 0)),
                  _full(lam_vecs.shape), _full((dv, 1))],
        out_specs=pl.BlockSpec((None, blk, dv), lambda bi, h, i: (bi, i, h)),
        out_shape=jax.ShapeDtypeStruct((b, t, wa), BF16),
        scratch_shapes=[pltpu.VMEM((2, dv + ones_rows, blk), F32), pltpu.VMEM((2, blk), F32),
                        pltpu.VMEM((2, blk, blk), F32), pltpu.VMEM((2, blk, blk), F32)],
        compiler_params=_params("parallel", "parallel", "arbitrary"),
        name="attn_prompt",
    )(slopes, qt, kbf, vt, lam_vecs, g_subln.reshape(dv, 1))


def _attn_sample_kernel(pt_ref, *refs, pps, heads, tq, past_len, page):
    k_pages = refs[:pps]
    v_pages = refs[pps:2 * pps]
    (q_ref, kn_ref, vn_ref, slope_ref, qidx_ref, qmask_ref, lam_ref, gsub_ref, o_ref,
     qbd_ref, kpad_ref, vpad_ref, m_ref, l_ref, acc_ref) = refs[2 * pps:]
    c = pl.program_id(1)
    hr = 2 * tq
    rows = heads * hr
    dv = q_ref.shape[-1] // heads

    slope = slope_ref[...]
    qidx = qidx_ref[...]
    lane = lax.broadcasted_iota(jnp.int32, (rows, page), 1)

    def fold(s, head_vals):
        m_old = m_ref[...]
        m_new = jnp.maximum(m_old, jnp.max(s, axis=-1, keepdims=True))
        p = jnp.exp(s - m_new)
        alpha = jnp.exp(m_old - m_new)
        l_ref[...] = alpha * l_ref[...] + jnp.sum(p, axis=-1, keepdims=True)
        pb = p.astype(BF16)
        for h in range(heads):
            r = slice(h * hr, (h + 1) * hr)
            pv = _dot(pb[r, :page], head_vals(0, h))
            for j in range(1, s.shape[1] // page):
                pv += _dot(pb[r, j * page:(j + 1) * page], head_vals(j, h))
            acc_ref[r, :] = alpha[r] * acc_ref[r, :] + pv
        m_ref[...] = m_new

    @pl.when(c == 0)
    def _():
        q = q_ref[...]
        qt = jnp.concatenate([q] * (rows // tq), axis=0)
        qbd_ref[...] = (qt * qmask_ref[...]).astype(BF16)
        m_ref[...] = jnp.full_like(m_ref, -jnp.inf)
        l_ref[...] = jnp.zeros_like(l_ref)
        acc_ref[...] = jnp.zeros_like(acc_ref)
        kpad_ref[...] = jnp.zeros_like(kpad_ref)
        vpad_ref[...] = jnp.zeros_like(vpad_ref)
        kpad_ref[0:tq, :] = kn_ref[...]
        vpad_ref[0:tq, :] = vn_ref[...]
        s = _dot_nt(qbd_ref[...], kpad_ref[...].astype(BF16))
        s = s - slope * (qidx - lane).astype(F32)
        s = jnp.where(lane <= qidx, s, -jnp.inf)
        fold(s, lambda j, h: vpad_ref[:, h * dv:(h + 1) * dv].astype(BF16))

    qbd = qbd_ref[...]
    qpos = past_len + qidx
    parts = []
    for j in range(pps):
        s = _dot(qbd, k_pages[j][...].astype(BF16))
        kpos = (c * pps + j) * page + lane
        parts.append(s - slope * (qpos - kpos).astype(F32))
    fold(jnp.concatenate(parts, axis=1),
         lambda j, h: v_pages[j][pl.ds(h, page, stride=heads), :].astype(BF16))

    @pl.when(c == pl.num_programs(1) - 1)
    def _():
        lam = _lambda_full(lam_ref)
        o = acc_ref[...] / l_ref[...]
        for h in range(heads):
            d = o[h * hr:h * hr + tq] - lam * o[h * hr + tq:(h + 1) * hr]
            d = _rmsnorm(d, gsub_ref[...]) * (1.0 - LAM_INIT)
            o_ref[:, h * dv:(h + 1) * dv] = d.astype(o_ref.dtype)


def _attn_sample(q, k_new, v_new, pool_k, pool_v, page_table, lam_vecs, g_subln, slopes_np, heads):
    db, tq, wa = q.shape
    n_phys, page = pool_k.shape[0], pool_k.shape[1]
    n_pages = page_table.shape[1]
    past_len = n_pages * page
    pps = math.gcd(SAMPLE_PAGES_PER_STEP, n_pages)
    dv = wa // heads
    dk = dv // 2
    rows = heads * 2 * tq
    pk = jnp.transpose(pool_k, (0, 2, 3, 4, 1)).reshape(n_phys, wa, page)
    pv = pool_v.reshape(n_phys, page * heads, dv)
    r_idx = np.arange(rows)
    slope_rows = jnp.asarray(np.repeat(np.asarray(slopes_np)[r_idx // (2 * tq)][:, None], page, 1), F32)
    qidx_rows = jnp.asarray(np.repeat((r_idx % tq)[:, None], page, 1), jnp.int32)
    qmask = jnp.asarray((r_idx // tq)[:, None] == (np.arange(wa) // dk)[None, :], F32)

    def page_spec(j, shape):
        return pl.BlockSpec((None,) + shape,
                            lambda b, c, pt: (pt[b * n_pages + c * pps + j], 0, 0))

    per_b = lambda b, c, pt: (b, 0, 0)
    const2 = lambda b, c, pt: (0, 0)
    grid_spec = pltpu.PrefetchScalarGridSpec(
        num_scalar_prefetch=1,
        grid=(db, n_pages // pps),
        in_specs=([page_spec(j, (wa, page)) for j in range(pps)]
                  + [page_spec(j, (page * heads, dv)) for j in range(pps)]
                  + [pl.BlockSpec((None, tq, wa), per_b)] * 3
                  + [pl.BlockSpec((rows, page), const2), pl.BlockSpec((rows, page), const2),
                     pl.BlockSpec((rows, wa), const2),
                     pl.BlockSpec(lam_vecs.shape, const2),
                     pl.BlockSpec((1, dv), const2)]),
        out_specs=pl.BlockSpec((None, tq, wa), per_b),
        scratch_shapes=[pltpu.VMEM((rows, wa), BF16),
                        pltpu.VMEM((page, wa), F32), pltpu.VMEM((page, wa), F32),
                        pltpu.VMEM((rows, 1), F32), pltpu.VMEM((rows, 1), F32),
                        pltpu.VMEM((rows, dv), F32)])
    return pl.pallas_call(
        functools.partial(_attn_sample_kernel, pps=pps, heads=heads, tq=tq,
                          past_len=past_len, page=page),
        grid_spec=grid_spec,
        out_shape=jax.ShapeDtypeStruct((db, tq, wa), F32),
        compiler_params=_params("parallel", "arbitrary"),
        name="attn_sample",
    )(page_table.reshape(-1), *([pk] * pps), *([pv] * pps), q, k_new, v_new,
      slope_rows, qidx_rows, qmask, lam_vecs, g_subln.reshape(1, dv))


def _s5_kernel(u_ref, x0_ref, bd_ref, cd_ref, a_ref, d_ref, y_ref, xf_ref,
               us_ref, xs_ref, ys_ref, st_ref, *, nbb, tbs, half_in, half_state):
    c = pl.program_id(1)

    @pl.when(c == 0)
    def _():
        st_ref[...] = x0_ref[...]

    sub, lanes = V7X_SUBLANES, V7X_LANES
    width = u_ref.shape[-1]
    n_us = width // lanes
    hs = half_state

    def batch_rows(tb, bb):
        return pl.ds(tb * sub * nbb + bb, sub, stride=nbb)

    for tb in range(tbs):
        for bb in range(nbb):
            ut = u_ref[tb, bb]
            for s in range(n_us):
                us_ref[s, batch_rows(tb, bb), :] = ut[:, s * lanes:(s + 1) * lanes]
    ub = jnp.concatenate([us_ref[s] for s in range(n_us)], axis=1).astype(BF16)
    for k in range(2):
        xs_ref[:, 2 * k * hs:2 * (k + 1) * hs] = _dot(
            ub[:, k * half_in:(k + 1) * half_in], bd_ref[k])

    for k in range(2):
        re0, im0 = 2 * k * hs, (2 * k + 1) * hs
        a_re = a_ref[2 * k]
        a_im = a_ref[2 * k + 1]

        def group(r, _):
            rows0 = pl.multiple_of(r * sub, sub)
            x_re = st_ref[pl.ds(rows0, sub), re0:re0 + hs]
            x_im = st_ref[pl.ds(rows0, sub), im0:im0 + hs]

            def step(t, carry):
                x_re, x_im = carry
                row = pl.multiple_of(t * nbb + rows0, sub)
                n_re = a_re * x_re - a_im * x_im + xs_ref[pl.ds(row, sub), re0:re0 + hs]
                n_im = a_re * x_im + a_im * x_re + xs_ref[pl.ds(row, sub), im0:im0 + hs]
                xs_ref[pl.ds(row, sub), re0:re0 + hs] = n_re
                xs_ref[pl.ds(row, sub), im0:im0 + hs] = n_im
                return n_re, n_im

            x_re, x_im = lax.fori_loop(0, tbs * sub, step, (x_re, x_im))
            st_ref[pl.ds(rows0, sub), re0:re0 + hs] = x_re
            st_ref[pl.ds(rows0, sub), im0:im0 + hs] = x_im
            return 0

        lax.fori_loop(0, nbb // sub, group, 0)

    for k in range(2):
        yk = _dot(xs_ref[:, 2 * k * hs:2 * (k + 1) * hs].astype(BF16), cd_ref[k])
        for s in range(yk.shape[1] // lanes):
            ys_ref[k * (yk.shape[1] // lanes) + s] = yk[:, s * lanes:(s + 1) * lanes]
    d = d_ref[...]
    for tb in range(tbs):
        for bb in range(nbb):
            yt = jnp.concatenate([ys_ref[s, batch_rows(tb, bb), :] for s in range(n_us)], axis=1)
            y_ref[tb, bb] = yt + d * u_ref[tb, bb]

    @pl.when(c == pl.num_programs(1) - 1)
    def _():
        xf_ref[...] = st_ref[...]


def _s5(u4, x0, bd, cd, a_coef, d_vec):
    n_tb, nb, sub, width = u4.shape
    nbb = min(nb, S5_BATCH_BLOCK)
    tbs = min(n_tb, max(1, S5_CHUNK_ROWS // (nbb * sub)))
    rows = tbs * nbb * sub
    n_state = x0.shape[1]
    tok = pl.BlockSpec((tbs, nbb, sub, width), lambda g, c: (c, g, 0, 0))
    st = pl.BlockSpec((nbb, n_state), lambda g, c: (g, 0))
    slabs = pltpu.VMEM((width // V7X_LANES, rows, V7X_LANES), F32)
    return pl.pallas_call(
        functools.partial(_s5_kernel, nbb=nbb, tbs=tbs, half_in=width // 2,
                          half_state=n_state // 4),
        grid=(nb // nbb, n_tb // tbs),
        in_specs=[tok, st, _full(bd.shape), _full(cd.shape), _full(a_coef.shape),
                  _full((1, width))],
        out_specs=[tok, st],
        out_shape=[jax.ShapeDtypeStruct(u4.shape, F32), jax.ShapeDtypeStruct(x0.shape, F32)],
        scratch_shapes=[slabs, pltpu.VMEM((rows, n_state), F32), slabs,
                        pltpu.VMEM((nbb, n_state), F32)],
        compiler_params=_params("parallel", "arbitrary"),
        name="s5_scan",
    )(u4, x0, bd, cd, a_coef, d_vec.reshape(1, width))


def _s5_weights(a_re, a_im, b_re, b_im, c_re, c_im, log_dt):
    g, p, ch = b_re.shape
    dt = jnp.exp(log_dt)[:, None]
    mag = jnp.exp(dt * a_re)
    abar_re, abar_im = mag * jnp.cos(dt * a_im), mag * jnp.sin(dt * a_im)
    den = a_re * a_re + a_im * a_im
    f_re = ((abar_re - 1.0) * a_re + abar_im * a_im) / den
    f_im = (abar_im * a_re - (abar_re - 1.0) * a_im) / den
    bbar_re = f_re[..., None] * b_re - f_im[..., None] * b_im
    bbar_im = f_re[..., None] * b_im + f_im[..., None] * b_re
    gh = g // 2
    eye = jnp.eye(gh, dtype=F32)

    def bd_half(w):
        return jnp.einsum('gpc,gh->gchp', w, eye).reshape(gh * ch, gh * p)

    def cd_half(w):
        return jnp.einsum('gcp,gh->gphc', w, eye).reshape(gh * p, gh * ch)

    bd = jnp.stack([jnp.concatenate([bd_half(bbar_re[k * gh:(k + 1) * gh]),
                                     bd_half(bbar_im[k * gh:(k + 1) * gh])], axis=1)
                    for k in range(2)]).astype(BF16)
    cd = jnp.stack([jnp.concatenate([cd_half(c_re[k * gh:(k + 1) * gh]),
                                     cd_half(-c_im[k * gh:(k + 1) * gh])], axis=0)
                    for k in range(2)]).astype(BF16)
    coef = [w[k * gh:(k + 1) * gh].reshape(1, gh * p) for k in range(2) for w in (abar_re, abar_im)]
    a_coef = jnp.broadcast_to(jnp.stack(coef), (4, V7X_SUBLANES, gh * p))
    return bd, cd, a_coef


def _state_pack(x_re, x_im):
    nb, g, p = x_re.shape
    hs = g * p // 2
    re = x_re.reshape(nb, 2, hs)
    im = x_im.reshape(nb, 2, hs)
    return jnp.stack([re[:, 0], im[:, 0], re[:, 1], im[:, 1]], axis=1).reshape(nb, 4 * hs)


def _state_unpack(x, g, p):
    nb = x.shape[0]
    hs = g * p // 2
    x = x.reshape(nb, 4, hs)
    re = jnp.stack([x[:, 0], x[:, 2]], axis=1).reshape(nb, g, p)
    im = jnp.stack([x[:, 1], x[:, 3]], axis=1).reshape(nb, g, p)
    return re, im


def _merge_kernel(x_ref, oa_ref, ys_ref, gmix_ref, wg_ref, bg_ref, wglu_ref, bglu_ref,
                  wpa_ref, wps_ref, wout_ref, gcross_ref, wcq_ref, x1_ref, qx_ref, *, qscale):
    x = x_ref[...]
    d = x.shape[-1]
    h = _rmsnorm(x, gmix_ref[...]).astype(BF16)
    gate = jax.nn.sigmoid(_dot(h, wg_ref[...]) + bg_ref[...])
    z = jax.nn.gelu(ys_ref[...].reshape(x.shape[0], -1))
    o_s = z * jax.nn.sigmoid(_dot(z.astype(BF16), wglu_ref[...]) + bglu_ref[...])
    pa = _dot(oa_ref[...].astype(BF16), wpa_ref[...])
    ps = _dot(o_s.astype(BF16), wps_ref[...])
    mix = gate[:, :d] * pa + gate[:, d:] * ps
    x1 = x + _dot(mix.astype(BF16), wout_ref[...])
    x1_ref[...] = x1
    hc = _rmsnorm(x1, gcross_ref[...]).astype(BF16)
    qx_ref[...] = (_dot(hc, wcq_ref[...]) * qscale).astype(qx_ref.dtype)


def _merge(x, oa, ys, ys_spec, grid, tok_spec, weights, qscale, inter_dtype):
    (gmix, wg, bg, wglu, bglu, wpa, wps, wout, gcross, wcq) = weights
    d = x.shape[-1]
    wx = wcq.shape[1]
    wspecs = [_full(w.shape) for w in weights]
    return pl.pallas_call(
        functools.partial(_merge_kernel, qscale=qscale),
        grid=grid,
        in_specs=[tok_spec(d), tok_spec(oa.shape[-1]), ys_spec] + wspecs,
        out_specs=[tok_spec(d), tok_spec(wx)],
        out_shape=[jax.ShapeDtypeStruct(x.shape, F32),
                   jax.ShapeDtypeStruct(x.shape[:-1] + (wx,), inter_dtype)],
        compiler_params=_params(*(("parallel",) * len(grid))),
        name="merge",
    )(x, oa, ys, *weights)


def _cross_kernel(q_ref, k_ref, v_ref, o_ref, *, heads):
    dx = q_ref.shape[-1] // heads
    n_mem = k_ref.shape[1] // heads
    for bb in range(q_ref.shape[0]):
        for h in range(heads):
            sl = slice(h * dx, (h + 1) * dx)
            rows = pl.ds(h, n_mem, stride=heads)
            q = q_ref[bb, :, sl].astype(BF16)
            s = _dot_nt(q, k_ref[bb, rows, :].astype(BF16))
            m = jnp.max(s, axis=-1, keepdims=True)
            p = jnp.exp(s - m)
            l = jnp.sum(p, axis=-1, keepdims=True)
            o = _dot(p.astype(BF16), v_ref[bb, rows, :].astype(BF16)) / l
            o_ref[bb, :, sl] = o.astype(o_ref.dtype)


def _cross(qx, mem_k, mem_v, heads, tm, bblk):
    b, t, wx = qx.shape
    rows, dx = mem_k.shape[1:]
    tok = pl.BlockSpec((bblk, tm, wx), lambda bi, i: (bi, i, 0))
    mem = pl.BlockSpec((bblk, rows, dx), lambda bi, i: (bi, 0, 0))
    return pl.pallas_call(
        functools.partial(_cross_kernel, heads=heads),
        grid=(b // bblk, t // tm),
        in_specs=[tok, mem, mem],
        out_specs=tok,
        out_shape=jax.ShapeDtypeStruct(qx.shape, qx.dtype),
        compiler_params=_params("parallel", "parallel"),
        name="cross_attn",
    )(qx, mem_k, mem_v)


def _ffn_kernel(x_ref, ca_ref, wco_ref, gffn_ref, wfi_ref, wfo_ref, gfin_ref, y_ref, *, dff):
    x2 = x_ref[...] + _dot(ca_ref[...].astype(BF16), wco_ref[...])
    h = _rmsnorm(x2, gffn_ref[...]).astype(BF16)
    ff = _dot(h, wfi_ref[...])
    hid = jax.nn.silu(ff[:, :dff]) * ff[:, dff:]
    x3 = x2 + _dot(hid.astype(BF16), wfo_ref[...])
    y_ref[...] = _rmsnorm(x3, gfin_ref[...])


def _ffn(x1, ca, weights, tm):
    (wco, gffn, wfi, wfo, gfin) = weights
    n, d = x1.shape
    tm = min(tm, n)
    tok = lambda w: pl.BlockSpec((tm, w), lambda i: (i, 0))
    return pl.pallas_call(
        functools.partial(_ffn_kernel, dff=wfo.shape[0]),
        grid=(n // tm,),
        in_specs=[tok(d), tok(ca.shape[-1])] + [_full(w.shape) for w in weights],
        out_specs=tok(d),
        out_shape=jax.ShapeDtypeStruct((n, d), F32),
        compiler_params=_params("parallel"),
        name="ffn",
    )(x1, ca, *weights)


def kernel(x_prompt, x_sample, mem_prompt, cache_k, cache_v, state_ssm_re, state_ssm_im, cache_mem_k, cache_mem_v, page_table, g_mix, w_in, b_gate, lambda_q1, lambda_k1, lambda_q2, lambda_k2, g_subln, ssm_a_re, ssm_a_im, ssm_b_re, ssm_b_im, ssm_c_re, ssm_c_im, ssm_d, ssm_log_dt, w_glu, b_glu, w_proj_a, w_proj_s, w_out, g_cross, g_mem, w_cq, w_mem_kv, w_co, g_ffn, w_ff_in, w_ff_out, g_final):
    assert g_mix.shape[0] == 1, "single-layer trunk"
    b, t, d = x_prompt.shape
    db, dt_len, _ = x_sample.shape
    heads = cache_k.shape[3]
    dk = cache_k.shape[5]
    dv = cache_v.shape[4]
    wa = heads * dv
    g, p, _ = ssm_b_re.shape[1:]
    ws = ssm_d.shape[1]
    n_mem, heads_x, dx = cache_mem_k.shape[2:]
    wx = heads_x * dx
    row2 = lambda v: v.reshape(1, -1)

    w_in0 = w_in[0]
    qs = dk ** -0.5
    w_q, w_k, w_v, w_u = (w_in0[:, :wa] * qs, w_in0[:, wa:2 * wa],
                          w_in0[:, 2 * wa:3 * wa], w_in0[:, 3 * wa:3 * wa + ws])
    w_gate = w_in0[:, 3 * wa + ws:].astype(BF16)
    w_kvu = jnp.concatenate([w_k, w_v, w_u], axis=1).astype(BF16)
    w_qvt = (w_q * LOG2E).T.astype(BF16)
    w_qkvu = jnp.concatenate([w_q, w_k, w_v, w_u], axis=1).astype(BF16)
    lam_vecs = jnp.concatenate([lambda_q1, lambda_k1, lambda_q2, lambda_k2], axis=0)
    slopes_np = [2.0 ** (-8.0 * (h + 1) / heads) for h in range(heads)]
    bd, cd, a_coef = _s5_weights(ssm_a_re[0], ssm_a_im[0], ssm_b_re[0], ssm_b_im[0],
                                 ssm_c_re[0], ssm_c_im[0], ssm_log_dt[0])
    merge_w = (row2(g_mix[0]), w_gate, row2(b_gate[0]), w_glu[0].astype(BF16), row2(b_glu[0]),
               w_proj_a[0].astype(BF16), w_proj_s[0].astype(BF16), w_out[0].astype(BF16),
               row2(g_cross[0]), w_cq[0].astype(BF16))
    ffn_w = (w_co[0].astype(BF16), row2(g_ffn[0]), w_ff_in[0].astype(BF16),
             w_ff_out[0].astype(BF16), row2(g_final))
    xscale = dx ** -0.5

    mkv = _norm_matmul(mem_prompt.reshape(b * n_mem, d), g_mem[0], w_mem_kv[0].astype(BF16), 512)
    mk = mkv[:, :wx].reshape(b, n_mem, wx)
    mv = mkv[:, wx:].reshape(b, n_mem, wx)
    k32, v32, kbf, u_p, qt, vt = _in_proj_prompt(x_prompt, g_mix[0], w_kvu, w_qvt, wa, ws, heads)
    oa_p = _attn_prompt(qt, kbf, vt, lam_vecs, g_subln[0], slopes_np, heads)
    zeros = jnp.zeros((b, g, p), F32)
    ys_p, xf_p = _s5(u_p, _state_pack(zeros, zeros), bd, cd, a_coef, ssm_d[0])
    tm = MERGE_TOKENS
    sub = V7X_SUBLANES
    tok3 = lambda w: pl.BlockSpec((None, tm, w), lambda bi, i: (bi, i, 0))
    ys_spec = pl.BlockSpec((tm // sub, None, sub, ws), lambda bi, i: (i, bi, 0, 0))
    x1_p, qx_p = _merge(x_prompt, oa_p, ys_p, ys_spec, (b, t // tm), tok3, merge_w, xscale, BF16)
    ca_p = _cross(qx_p, mk.reshape(b, n_mem * heads_x, dx), mv.reshape(b, n_mem * heads_x, dx),
                  heads_x, CROSS_TOKENS, 1)
    y_prompt = _ffn(x1_p.reshape(b * t, d), ca_p.reshape(b * t, wx), ffn_w, FFN_TOKENS).reshape(b, t, d)
    srp, sip = _state_unpack(xf_p, g, p)

    n_s = db * dt_len
    q_s, k_s, v_s, u_s = _in_proj_sample(x_sample.reshape(n_s, d), g_mix[0], w_qkvu, wa, ws)
    oa_s = _attn_sample(q_s.reshape(db, dt_len, wa), k_s.reshape(db, dt_len, wa),
                        v_s.reshape(db, dt_len, wa), cache_k[0], cache_v[0], page_table,
                        lam_vecs, g_subln[0], slopes_np, heads)
    assert dt_len == sub, "sample rows (batch, step) are the S5 row order only for 8 new tokens"
    ys_s, xf_s = _s5(u_s.reshape(1, db, sub, ws),
                     _state_pack(state_ssm_re[0], state_ssm_im[0]), bd, cd, a_coef, ssm_d[0])
    ys_s = ys_s.reshape(n_s, ws)
    tok2 = lambda w: pl.BlockSpec((n_s, w), lambda i: (0, 0))
    x1_s, qx_s = _merge(x_sample.reshape(n_s, d), oa_s.reshape(n_s, wa), ys_s, tok2(ws), (1,),
                        tok2, merge_w, xscale, F32)
    ca_s = _cross(qx_s.reshape(db, dt_len, wx), cache_mem_k[0].reshape(db, n_mem * heads_x, dx),
                  cache_mem_v[0].reshape(db, n_mem * heads_x, dx), heads_x, dt_len,
                  math.gcd(db, CROSS_SAMPLE_BATCHES))
    y_sample = _ffn(x1_s, ca_s.reshape(n_s, wx), ffn_w, FFN_TOKENS).reshape(db, dt_len, d)
    srs, sis = _state_unpack(xf_s, g, p)

    return (y_prompt, y_sample,
            k32.reshape(1, b, t, heads, 2, dk), v32.reshape(1, b, t, heads, dv),
            k_s.reshape(1, db, dt_len, heads, 2, dk), v_s.reshape(1, db, dt_len, heads, dv),
            srp[None], sip[None], srs[None], sis[None],
            mk.reshape(1, b, n_mem, heads_x, dx), mv.reshape(1, b, n_mem, heads_x, dx))
```
